```python
import math
import jax, jax.numpy as jnp
from jax import lax
import numpy as np

D_MODEL = 1024
BATCH = 4
SEQ = 8192
DEPTH = 4

MEM_LEN = 256
S5_WIDTH = D_MODEL // 2
S5_GROUP_CH = 16
S5_GROUPS = S5_WIDTH // S5_GROUP_CH
S5_STATE = 64
RET_HEADS = 4
RET_DK = (D_MODEL - S5_WIDTH) // RET_HEADS
RET_DV = RET_DK
RET_WIDTH = RET_HEADS * RET_DV
RET_CHUNK = 128
MIX_WIDTH = S5_WIDTH + RET_WIDTH
IN_WIDTH = S5_WIDTH + 4 * RET_WIDTH
X_HEADS = 4
X_HEAD_DIM = D_MODEL // X_HEADS
D_FF = ((8 * D_MODEL // 3 + 255) // 256) * 256
ROPE_BASE = 10000.0
EPS = 1e-6

kernel_name = "hybrid_s5_retention_memxattn_swiglu"


def rms_norm(x, g):
    xf = x.astype(jnp.float32)
    y = xf * lax.rsqrt(jnp.mean(xf * xf, axis=-1, keepdims=True) + EPS)
    return (y * g.astype(jnp.float32)).astype(x.dtype)


def _complex_combine(left, right):
    ar1, ai1, br1, bi1 = left
    ar2, ai2, br2, bi2 = right
    ar = ar2 * ar1 - ai2 * ai1
    ai = ar2 * ai1 + ai2 * ar1
    br = ar2 * br1 - ai2 * bi1 + br2
    bi = ar2 * bi1 + ai2 * br1 + bi2
    return (ar, ai, br, bi)


def s5_mixer(u, lam_re, lam_im, log_step, b_re, b_im, c_re, c_im, d_skip, w_glu, b_glu):
    bsz, seq, _ = u.shape
    uf = u.astype(jnp.float32)
    ug = uf.reshape(bsz, seq, S5_GROUPS, S5_GROUP_CH)
    lr = lam_re.astype(jnp.float32)
    li = lam_im.astype(jnp.float32)
    step = jnp.exp(log_step.astype(jnp.float32))[:, None]
    mag = jnp.exp(lr * step)
    abar_r = mag * jnp.cos(li * step)
    abar_i = mag * jnp.sin(li * step)
    den = lr * lr + li * li
    f_r = ((abar_r - 1.0) * lr + abar_i * li) / den
    f_i = (abar_i * lr - (abar_r - 1.0) * li) / den
    br = b_re.astype(jnp.float32)
    bi = b_im.astype(jnp.float32)
    bb_r = f_r[..., None] * br - f_i[..., None] * bi
    bb_i = f_r[..., None] * bi + f_i[..., None] * br
    xr = jnp.einsum('blgh,gph->blgp', ug, bb_r)
    xi = jnp.einsum('blgh,gph->blgp', ug, bb_i)
    a_r = jnp.broadcast_to(abar_r[None, None], (1, seq) + abar_r.shape)
    a_i = jnp.broadcast_to(abar_i[None, None], (1, seq) + abar_i.shape)
    _, _, sr, si = lax.associative_scan(_complex_combine, (a_r, a_i, xr, xi), axis=1)
    y = (jnp.einsum('gnp,blgp->blgn', c_re.astype(jnp.float32), sr)
         - jnp.einsum('gnp,blgp->blgn', c_im.astype(jnp.float32), si))
    y = y.reshape(bsz, seq, S5_WIDTH) + d_skip.astype(jnp.float32) * uf
    z = jax.nn.gelu(y, approximate=False)
    z = z * jax.nn.sigmoid(z @ w_glu.astype(jnp.float32) + b_glu.astype(jnp.float32))
    return z.astype(u.dtype)


def rotary(x, positions):
    half = x.shape[-1] // 2
    inv_freq = 1.0 / (ROPE_BASE ** (jnp.arange(half, dtype=jnp.float32) / half))
    ang = positions.astype(jnp.float32)[..., None] * inv_freq
    cos = jnp.cos(ang)[:, :, None, :]
    sin = jnp.sin(ang)[:, :, None, :]
    xf = x.astype(jnp.float32)
    x1, x2 = xf[..., :half], xf[..., half:]
    return jnp.concatenate([x1 * cos - x2 * sin, x2 * cos + x1 * sin], axis=-1)


def retention(q, k, v, gate, positions, out_gain):
    bsz, seq, _ = q.shape
    nc = seq // RET_CHUNK
    qh = rotary(q.reshape(bsz, seq, RET_HEADS, RET_DK), positions)
    kh = rotary(k.reshape(bsz, seq, RET_HEADS, RET_DK), positions) * (RET_DK ** -0.5)
    vh = v.astype(jnp.float32).reshape(bsz, seq, RET_HEADS, RET_DV)
    qc = qh.reshape(bsz, nc, RET_CHUNK, RET_HEADS, RET_DK)
    kc = kh.reshape(bsz, nc, RET_CHUNK, RET_HEADS, RET_DK)
    vc = vh.reshape(bsz, nc, RET_CHUNK, RET_HEADS, RET_DV)
    lg = jnp.log1p(-jnp.exp2(-5.0 - jnp.arange(RET_HEADS, dtype=jnp.float32)))
    idx = jnp.arange(RET_CHUNK, dtype=jnp.float32)
    diff = idx[:, None] - idx[None, :]
    dmask = jnp.where(diff[None] >= 0,
                      jnp.exp(jnp.maximum(diff, 0.0)[None] * lg[:, None, None]), 0.0)
    scores = jnp.einsum('bnchd,bnshd->bnhcs', qc, kc) * dmask
    inner = jnp.einsum('bnhcs,bnshe->bnche', scores, vc)
    zeta = jnp.exp((RET_CHUNK - 1.0 - idx)[:, None] * lg[None, :])
    kv = jnp.einsum('bnshd,bnshe,sh->nbhde', kc, vc, zeta)
    g_chunk = jnp.exp(RET_CHUNK * lg)[None, :, None, None]

    def chunk_step(state, kv_j):
        return g_chunk * state + kv_j, state

    init = jnp.zeros((bsz, RET_HEADS, RET_DK, RET_DV), jnp.float32)
    _, r_prev = lax.scan(chunk_step, init, kv)
    xi = jnp.exp((idx + 1.0)[:, None] * lg[None, :])
    cross = jnp.einsum('bnchd,nbhde->bnche', qc, r_prev) * xi[None, None, :, :, None]
    o = (inner + cross).reshape(bsz, seq, RET_HEADS, RET_DV)
    mu = jnp.mean(o, axis=-1, keepdims=True)
    var = jnp.mean(jnp.square(o - mu), axis=-1, keepdims=True)
    o = (o - mu) * lax.rsqrt(var + EPS) * out_gain.astype(jnp.float32)
    o = o.reshape(bsz, seq, RET_WIDTH) * jax.nn.silu(gate.astype(jnp.float32))
    return o.astype(q.dtype)


def memory_cross_attention(h, m, wq, wk, wv, wo):
    bsz, seq, _ = h.shape
    q = (h @ wq).reshape(bsz, seq, X_HEADS, X_HEAD_DIM)
    k = (m @ wk).reshape(bsz, m.shape[1], X_HEADS, X_HEAD_DIM)
    v = (m @ wv).reshape(bsz, m.shape[1], X_HEADS, X_HEAD_DIM)
    s = jnp.einsum('blhd,bmhd->bhlm', q.astype(jnp.float32), k.astype(jnp.float32)) * (X_HEAD_DIM ** -0.5)
    p = jax.nn.softmax(s, axis=-1)
    o = jnp.einsum('bhlm,bmhd->blhd', p, v.astype(jnp.float32)).astype(h.dtype)
    return o.reshape(bsz, seq, D_MODEL) @ wo


def swiglu(h, w_gate, w_up, w_down):
    return (jax.nn.silu(h @ w_gate) * (h @ w_up)) @ w_down


def setup_inputs(seed: int = 0) -> dict:
    key = jax.random.key(seed)
    ks = iter(jax.random.split(key, 40))

    def nrm(shape, scale):
        return jax.random.normal(next(ks), shape, jnp.float32) * scale

    def gain(shape):
        return 1.0 + nrm(shape, 0.02)

    res_scale = 1.0 / math.sqrt(2.0 * DEPTH)
    x = nrm((BATCH, SEQ, D_MODEL), 1.0)
    mem = nrm((BATCH, MEM_LEN, D_MODEL), 1.0)
    positions = jnp.broadcast_to(jnp.arange(SEQ, dtype=jnp.int32)[None, :], (BATCH, SEQ))
    n_idx = jnp.arange(S5_STATE, dtype=jnp.float32)
    lam_re = -0.5 + nrm((DEPTH, S5_GROUPS, S5_STATE), 0.01)
    lam_im = math.pi * n_idx[None, None, :] + nrm((DEPTH, S5_GROUPS, S5_STATE), 0.01)
    log_step = jax.random.uniform(next(ks), (DEPTH, S5_GROUPS), jnp.float32,
                                  math.log(1e-3), math.log(1e-1))
    return {
        "x": x,
        "mem": mem,
        "positions": positions,
        "norm_mix": gain((DEPTH, D_MODEL)),
        "w_in": nrm((DEPTH, D_MODEL, IN_WIDTH), D_MODEL ** -0.5),
        "s5_lambda_re": lam_re,
        "s5_lambda_im": lam_im,
        "s5_log_step": log_step,
        "s5_b_re": nrm((DEPTH, S5_GROUPS, S5_STATE, S5_GROUP_CH), (2 * S5_GROUP_CH) ** -0.5),
        "s5_b_im": nrm((DEPTH, S5_GROUPS, S5_STATE, S5_GROUP_CH), (2 * S5_GROUP_CH) ** -0.5),
        "s5_c_re": nrm((DEPTH, S5_GROUPS, S5_GROUP_CH, S5_STATE), (2 * S5_STATE) ** -0.5),
        "s5_c_im": nrm((DEPTH, S5_GROUPS, S5_GROUP_CH, S5_STATE), (2 * S5_STATE) ** -0.5),
        "s5_d": nrm((DEPTH, S5_WIDTH), 1.0),
        "s5_w_glu": nrm((DEPTH, S5_WIDTH, S5_WIDTH), S5_WIDTH ** -0.5),
        "s5_b_glu": nrm((DEPTH, S5_WIDTH), 0.01),
        "s5_out_norm": gain((DEPTH, S5_WIDTH)),
        "ret_out_norm": gain((DEPTH, RET_HEADS, RET_DV)),
        "w_out": nrm((DEPTH, MIX_WIDTH, D_MODEL), MIX_WIDTH ** -0.5 * res_scale),
        "norm_cross": gain((DEPTH, D_MODEL)),
        "norm_mem": gain((DEPTH, D_MODEL)),
        "w_cq": nrm((DEPTH, D_MODEL, D_MODEL), D_MODEL ** -0.5),
        "w_ck": nrm((DEPTH, D_MODEL, D_MODEL), D_MODEL ** -0.5),
        "w_cv": nrm((DEPTH, D_MODEL, D_MODEL), D_MODEL ** -0.5),
        "w_co": nrm((DEPTH, D_MODEL, D_MODEL), D_MODEL ** -0.5 * res_scale),
        "norm_ffn": gain((DEPTH, D_MODEL)),
        "w_gate": nrm((DEPTH, D_MODEL, D_FF), D_MODEL ** -0.5),
        "w_up": nrm((DEPTH, D_MODEL, D_FF), D_MODEL ** -0.5),
        "w_down": nrm((DEPTH, D_FF, D_MODEL), D_FF ** -0.5 * res_scale),
        "norm_final": gain((D_MODEL,)),
    }


def reference(x, mem, positions, norm_mix, w_in, s5_lambda_re, s5_lambda_im, s5_log_step,
              s5_b_re, s5_b_im, s5_c_re, s5_c_im, s5_d, s5_w_glu, s5_b_glu, s5_out_norm,
              ret_out_norm, w_out, norm_cross, norm_mem, w_cq, w_ck, w_cv, w_co,
              norm_ffn, w_gate, w_up, w_down, norm_final):
    split_at = [S5_WIDTH, S5_WIDTH + RET_WIDTH, S5_WIDTH + 2 * RET_WIDTH, S5_WIDTH + 3 * RET_WIDTH]
    for l in range(DEPTH):
        h = rms_norm(x, norm_mix[l])
        proj = h @ w_in[l]
        u, q, k, v, g = jnp.split(proj, split_at, axis=-1)
        y_ssm = s5_mixer(u, s5_lambda_re[l], s5_lambda_im[l], s5_log_step[l],
                         s5_b_re[l], s5_b_im[l], s5_c_re[l], s5_c_im[l],
                         s5_d[l], s5_w_glu[l], s5_b_glu[l])
        y_ssm = rms_norm(y_ssm, s5_out_norm[l])
        y_ret = retention(q, k, v, g, positions, ret_out_norm[l])
        x = x + jnp.concatenate([y_ssm, y_ret], axis=-1) @ w_out[l]
        h = rms_norm(x, norm_cross[l])
        m = rms_norm(mem, norm_mem[l])
        x = x + memory_cross_attention(h, m, w_cq[l], w_ck[l], w_cv[l], w_co[l])
        h = rms_norm(x, norm_ffn[l])
        x = x + swiglu(h, w_gate[l], w_up[l], w_down[l])
    return rms_norm(x, norm_final)
```

```python
import functools
import math

import jax
import jax.numpy as jnp
from jax import lax
from jax.experimental import pallas as pl
from jax.experimental.pallas import tpu as pltpu

D_MODEL = 1024
DEPTH = 4
MEM_LEN = 256
S5_WIDTH = 512
S5_GROUP_CH = 16
S5_GROUPS = 32
S5_STATE = 64
RET_HEADS = 4
RET_DK = 128
RET_WIDTH = 512
IN_WIDTH = S5_WIDTH + 4 * RET_WIDTH
X_HEADS = 4
X_HEAD_DIM = 256
D_FF = 2816
ROPE_BASE = 10000.0
EPS = 1e-6

LANES = 128
S5_KT = S5_WIDTH // LANES
S5_GPT = LANES // S5_GROUP_CH
S5_SLAB = S5_GPT * S5_STATE
S5_TC = 128
S5_PITCH = S5_TC + 8
RET_CHUNK = 256
RET_TR = 1024
TM = 512
FF_SPLITS = ((0, 1536), (1536, D_FF))

BF16 = jnp.bfloat16
F32 = jnp.float32
VMEM_LIMIT = 48 * 1024 * 1024


def _dot(a, b):
    return jnp.dot(a, b, preferred_element_type=F32)


def _rms(x, g):
    ms = jnp.mean(x * x, axis=-1, keepdims=True)
    return x * lax.rsqrt(ms + EPS) * g


def _const_spec(shape):
    nd = len(shape)
    return pl.BlockSpec(shape, lambda *_: (0,) * nd, pipeline_mode=pl.Buffered(1))


def _params(*sem):
    return pltpu.CompilerParams(dimension_semantics=sem, vmem_limit_bytes=VMEM_LIMIT)


def _rope_kernel(pos_ref, invf_ref, sign_ref, c_ref, s_ref):
    ang = pos_ref[...] * invf_ref[...]
    c_ref[...] = jnp.cos(ang)
    s_ref[...] = jnp.sin(ang) * sign_ref[...]


def _rope_tables(positions):
    n = positions.size
    half = RET_DK // 2
    inv_freq = 1.0 / (ROPE_BASE ** (jnp.arange(half, dtype=F32) / half))
    invf2 = jnp.concatenate([inv_freq, inv_freq]).reshape(1, RET_DK)
    sign = jnp.concatenate([-jnp.ones((half,), F32), jnp.ones((half,), F32)]).reshape(1, RET_DK)
    pos = positions.astype(F32).reshape(n, 1)
    t = 1024
    return pl.pallas_call(
        _rope_kernel,
        grid=(n // t,),
        in_specs=[pl.BlockSpec((t, 1), lambda i: (i, 0)),
                  pl.BlockSpec((1, RET_DK), lambda i: (0, 0)),
                  pl.BlockSpec((1, RET_DK), lambda i: (0, 0))],
        out_specs=[pl.BlockSpec((t, RET_DK), lambda i: (i, 0))] * 2,
        out_shape=[jax.ShapeDtypeStruct((n, RET_DK), F32)] * 2,
        compiler_params=_params("arbitrary"),
        name="rope_tables",
    )(pos, invf2, sign)


def _s5_disc_kernel(lr_ref, li_ref, ls_ref, brt_ref, bit_ref, ar_ref, ai_ref, bbr_ref, bbi_ref):
    lr = lr_ref[0]
    li = li_ref[0]
    step = jnp.exp(ls_ref[0])
    mag = jnp.exp(lr * step)
    a_r = mag * jnp.cos(li * step)
    a_i = mag * jnp.sin(li * step)
    den = lr * lr + li * li
    f_r = ((a_r - 1.0) * lr + a_i * li) / den
    f_i = (a_i * lr - (a_r - 1.0) * li) / den
    br = brt_ref[0]
    bi = bit_ref[0]
    ar_ref[0] = a_r
    ai_ref[0] = a_i
    bbr_ref[0] = f_r * br - f_i * bi
    bbi_ref[0] = f_r * bi + f_i * br


def _s5_discretise(lam_re, lam_im, log_step, b_re, b_im):
    g, p, h = S5_GROUPS, S5_STATE, S5_GROUP_CH
    lr = lam_re.reshape(DEPTH, g, 1, p)
    li = lam_im.reshape(DEPTH, g, 1, p)
    ls = log_step.reshape(DEPTH, g, 1, 1)
    brt = jnp.swapaxes(b_re, -1, -2)
    bit = jnp.swapaxes(b_im, -1, -2)
    vec = pl.BlockSpec((1, g, 1, p), lambda l: (l, 0, 0, 0))
    mat = pl.BlockSpec((1, g, h, p), lambda l: (l, 0, 0, 0))
    return pl.pallas_call(
        _s5_disc_kernel,
        grid=(DEPTH,),
        in_specs=[vec, vec, pl.BlockSpec((1, g, 1, 1), lambda l: (l, 0, 0, 0)), mat, mat],
        out_specs=[vec, vec, mat, mat],
        out_shape=[jax.ShapeDtypeStruct((DEPTH, g, 1, p), F32)] * 2
        + [jax.ShapeDtypeStruct((DEPTH, g, h, p), F32)] * 2,
        compiler_params=_params("arbitrary"),
        name="s5_discretise",
    )(lr, li, ls, brt, bit)


def _block_diag_tiles(blocks):
    d, _, r, c = blocks.shape
    b = blocks.reshape(d, S5_KT, S5_GPT, r, c)
    eye = jnp.eye(S5_GPT, dtype=blocks.dtype)
    out = b[:, :, :, :, None, :] * eye[None, None, :, None, :, None]
    return out.reshape(d, S5_KT, S5_GPT * r, S5_GPT * c)


def _memkv_kernel(mem_ref, g_ref, wk_ref, wv_ref, k_ref, v_ref):
    m = _rms(mem_ref[...], g_ref[0]).astype(BF16)
    k_ref[0] = _dot(m, wk_ref[0]).astype(BF16)
    v_ref[0] = _dot(m, wv_ref[0]).astype(BF16)


def _mem_kv(mem2d, norm_mem, w_ck, w_cv):
    rows = mem2d.shape[0]
    wspec = pl.BlockSpec((1, D_MODEL, D_MODEL), lambda l: (l, 0, 0))
    ospec = pl.BlockSpec((1, rows, D_MODEL), lambda l: (l, 0, 0))
    return pl.pallas_call(
        _memkv_kernel,
        grid=(DEPTH,),
        in_specs=[pl.BlockSpec((rows, D_MODEL), lambda l: (0, 0)),
                  pl.BlockSpec((1, 1, D_MODEL), lambda l: (l, 0, 0)), wspec, wspec],
        out_specs=[ospec, ospec],
        out_shape=[jax.ShapeDtypeStruct((DEPTH, rows, D_MODEL), BF16)] * 2,
        compiler_params=_params("arbitrary"),
        name="mem_kv",
    )(mem2d, norm_mem.reshape(DEPTH, 1, D_MODEL), w_ck, w_cv)


def _in_proj_kernel(x_ref, g_ref, w_ref, c_ref, s_ref, u_ref, q_ref, k_ref, v_ref, gate_ref):
    h = _rms(x_ref[...], g_ref[...]).astype(BF16)
    cos = c_ref[...]
    sin = s_ref[...]
    u_ref[...] = _dot(h, w_ref[:, 0:S5_WIDTH])
    for dst, off, scale in ((q_ref, S5_WIDTH, None), (k_ref, S5_WIDTH + RET_WIDTH, RET_DK ** -0.5)):
        t = _dot(h, w_ref[:, off:off + RET_WIDTH])
        for hd in range(RET_HEADS):
            th = t[:, hd * RET_DK:(hd + 1) * RET_DK]
            r = th * cos + pltpu.roll(th, RET_DK // 2, axis=1) * sin
            if scale is not None:
                r = r * scale
            dst[:, hd * RET_DK:(hd + 1) * RET_DK] = r.astype(BF16)
    off = S5_WIDTH + 2 * RET_WIDTH
    v_ref[...] = _dot(h, w_ref[:, off:off + RET_WIDTH]).astype(BF16)
    gate_ref[...] = _dot(h, w_ref[:, off + RET_WIDTH:off + 2 * RET_WIDTH])


def _in_proj(x2d, g, w_bf16, cos2, sin2):
    n = x2d.shape[0]
    row = lambda w: pl.BlockSpec((TM, w), lambda i: (i, 0))
    return pl.pallas_call(
        _in_proj_kernel,
        grid=(n // TM,),
        in_specs=[row(D_MODEL), _const_spec((1, D_MODEL)), _const_spec((D_MODEL, IN_WIDTH)),
                  row(RET_DK), row(RET_DK)],
        out_specs=[row(S5_WIDTH), row(RET_WIDTH), row(RET_WIDTH), row(RET_WIDTH), row(RET_WIDTH)],
        out_shape=[jax.ShapeDtypeStruct((n, S5_WIDTH), F32),
                   jax.ShapeDtypeStruct((n, RET_WIDTH), BF16),
                   jax.ShapeDtypeStruct((n, RET_WIDTH), BF16),
                   jax.ShapeDtypeStruct((n, RET_WIDTH), BF16),
                   jax.ShapeDtypeStruct((n, RET_WIDTH), F32)],
        compiler_params=_params("arbitrary"),
        name="in_proj",
    )(x2d, g.reshape(1, D_MODEL), w_bf16, cos2, sin2)


def _s5_kernel(nb, u_ref, wb_ref, ar_ref, ai_ref, wcr_ref, wci_ref, d_ref, wglu_ref, bglu_ref,
               gout_ref, o_ref, xs_ref, st_ref, y_ref):
    nslab = S5_SLAB // LANES

    @pl.when(pl.program_id(0) == 0)
    def _():
        st_ref[...] = jnp.zeros_like(st_ref)

    for kt in range(S5_KT):
        lanes = slice(kt * LANES, (kt + 1) * LANES)
        ub = u_ref[:, :, lanes].reshape(nb * S5_TC, LANES).astype(BF16)
        xb = _dot(ub, wb_ref[kt])
        for b in range(nb):
            for j in range(2 * nslab):
                xs_ref[j, b * S5_PITCH:b * S5_PITCH + S5_TC, :] = (
                    xb[b * S5_TC:(b + 1) * S5_TC, j * LANES:(j + 1) * LANES])

        a_r = [jnp.broadcast_to(ar_ref[kt, :, j * LANES:(j + 1) * LANES], (nb, LANES)) for j in range(nslab)]
        a_i = [jnp.broadcast_to(ai_ref[kt, :, j * LANES:(j + 1) * LANES], (nb, LANES)) for j in range(nslab)]
        s_r0 = tuple(st_ref[kt, j] for j in range(nslab))
        s_i0 = tuple(st_ref[kt, nslab + j] for j in range(nslab))

        def step(t, carry):
            s_r, s_i = carry
            n_r, n_i = [], []
            for j in range(nslab):
                rows = pl.ds(t, nb, stride=S5_PITCH)
                x_r = xs_ref[j, rows, :]
                x_i = xs_ref[nslab + j, rows, :]
                r = a_r[j] * s_r[j] - a_i[j] * s_i[j] + x_r
                i = a_r[j] * s_i[j] + a_i[j] * s_r[j] + x_i
                xs_ref[j, rows, :] = r
                xs_ref[nslab + j, rows, :] = i
                n_r.append(r)
                n_i.append(i)
            return tuple(n_r), tuple(n_i)

        s_r, s_i = lax.fori_loop(0, S5_TC, step, (s_r0, s_i0), unroll=8)
        for j in range(nslab):
            st_ref[kt, j] = s_r[j]
            st_ref[kt, nslab + j] = s_i[j]

        def gather(first):
            return jnp.concatenate(
                [jnp.concatenate([xs_ref[first + j, b * S5_PITCH:b * S5_PITCH + S5_TC, :]
                                  for j in range(nslab)], axis=1) for b in range(nb)],
                axis=0).astype(BF16)

        yk = _dot(gather(0), wcr_ref[kt]) - _dot(gather(nslab), wci_ref[kt])
        y_ref[:, :, lanes] = yk.reshape(nb, S5_TC, LANES)

    u = u_ref[...].reshape(nb * S5_TC, S5_WIDTH)
    y = y_ref[...].reshape(nb * S5_TC, S5_WIDTH) + d_ref[...] * u
    z = 0.5 * y * (1.0 + lax.erf(y * math.sqrt(0.5)))
    z = z * jax.nn.sigmoid(_dot(z.astype(BF16), wglu_ref[...]) + bglu_ref[...])
    o_ref[...] = _rms(z, gout_ref[...]).astype(BF16).reshape(nb, S5_TC, S5_WIDTH)


def _s5_mixer(u3, wb, a_r, a_i, wcr, wci, d, wglu, bglu, gout):
    nb, seq, _ = u3.shape
    nslab2 = 2 * S5_SLAB // LANES
    tok = pl.BlockSpec((nb, S5_TC, S5_WIDTH), lambda c: (0, c, 0))
    return pl.pallas_call(
        functools.partial(_s5_kernel, nb),
        grid=(seq // S5_TC,),
        in_specs=[tok,
                  _const_spec((S5_KT, LANES, 2 * S5_SLAB)),
                  _const_spec((S5_KT, 1, S5_SLAB)), _const_spec((S5_KT, 1, S5_SLAB)),
                  _const_spec((S5_KT, S5_SLAB, LANES)), _const_spec((S5_KT, S5_SLAB, LANES)),
                  _const_spec((1, S5_WIDTH)), _const_spec((S5_WIDTH, S5_WIDTH)),
                  _const_spec((1, S5_WIDTH)), _const_spec((1, S5_WIDTH))],
        out_specs=tok,
        out_shape=jax.ShapeDtypeStruct((nb, seq, S5_WIDTH), BF16),
        scratch_shapes=[pltpu.VMEM((nslab2, nb * S5_PITCH, LANES), F32),
                        pltpu.VMEM((S5_KT, nslab2, nb, LANES), F32),
                        pltpu.VMEM((nb, S5_TC, S5_WIDTH), F32)],
        compiler_params=_params("arbitrary"),
        name="s5_mixer",
    )(u3, wb, a_r, a_i, wcr, wci, d, wglu, bglu, gout)


def _ret_tables():
    lg = jnp.log1p(-jnp.exp2(-5.0 - jnp.arange(RET_HEADS, dtype=F32)))
    idx = jnp.arange(RET_CHUNK, dtype=F32)
    diff = idx[:, None] - idx[None, :]
    dmask = jnp.where(diff[None] >= 0, jnp.exp(jnp.maximum(diff, 0.0)[None] * lg[:, None, None]), 0.0)
    zeta = jnp.exp((RET_CHUNK - 1.0 - idx)[None, :] * lg[:, None])
    xi = jnp.exp((idx + 1.0)[None, :] * lg[:, None])
    gch = jnp.exp(RET_CHUNK * lg)
    bc = lambda a: jnp.broadcast_to(a[:, :, None], (RET_HEADS, RET_CHUNK, RET_DK))
    return dmask, bc(zeta), bc(xi), jnp.broadcast_to(gch[:, None, None], (RET_HEADS, 1, RET_DK))


def _ret_kernel(q_ref, k_ref, v_ref, g_ref, dm_ref, zeta_ref, xi_ref, gch_ref, gain_ref, o_ref, st_ref):
    @pl.when(pl.program_id(1) == 0)
    def _():
        st_ref[...] = jnp.zeros_like(st_ref)

    def chunk(c, carry):
        rows = pl.ds(pl.multiple_of(c * RET_CHUNK, RET_CHUNK), RET_CHUNK)
        for hd in range(RET_HEADS):
            lanes = slice(hd * RET_DK, (hd + 1) * RET_DK)
            qh = q_ref[rows, lanes]
            kh = k_ref[rows, lanes]
            vh = v_ref[rows, lanes]
            s = lax.dot_general(qh, kh, (((1,), (1,)), ((), ())), preferred_element_type=F32)
            s = s * dm_ref[hd]
            inner = _dot(s.astype(BF16), vh)
            state = st_ref[hd]
            cross = _dot(qh, state.astype(BF16)) * xi_ref[hd]
            kz = (kh.astype(F32) * zeta_ref[hd]).astype(BF16)
            kv = lax.dot_general(kz, vh, (((0,), (0,)), ((), ())), preferred_element_type=F32)
            st_ref[hd] = gch_ref[hd] * state + kv
            o = inner + cross
            mu = jnp.mean(o, axis=-1, keepdims=True)
            oc = o - mu
            var = jnp.mean(oc * oc, axis=-1, keepdims=True)
            o = oc * lax.rsqrt(var + EPS) * gain_ref[hd]
            o = o * jax.nn.silu(g_ref[rows, lanes])
            o_ref[rows, lanes] = o.astype(BF16)
        return carry

    lax.fori_loop(0, RET_TR // RET_CHUNK, chunk, 0)


def _retention(q, k, v, gate, tables, gain, nb, seq):
    dmask, zeta, xi, gch = tables
    steps = seq // RET_TR
    tok = pl.BlockSpec((RET_TR, RET_WIDTH), lambda b, c: (b * steps + c, 0))
    return pl.pallas_call(
        _ret_kernel,
        grid=(nb, steps),
        in_specs=[tok, tok, tok, tok,
                  _const_spec((RET_HEADS, RET_CHUNK, RET_CHUNK)),
                  _const_spec((RET_HEADS, RET_CHUNK, RET_DK)),
                  _const_spec((RET_HEADS, RET_CHUNK, RET_DK)),
                  _const_spec((RET_HEADS, 1, RET_DK)),
                  _const_spec((RET_HEADS, 1, RET_DK))],
        out_specs=tok,
        out_shape=jax.ShapeDtypeStruct((nb * seq, RET_WIDTH), BF16),
        scratch_shapes=[pltpu.VMEM((RET_HEADS, RET_DK, RET_DK), F32)],
        compiler_params=_params("arbitrary", "arbitrary"),
        name="retention",
    )(q, k, v, gate, dmask, zeta, xi, gch, gain.reshape(RET_HEADS, 1, RET_DK))


def _mix_out_kernel(x_ref, ys_ref, yr_ref, wo1_ref, wo2_ref, g_ref, wq_ref, k_ref, v_ref, wco_ref, o_ref):
    x = x_ref[...] + _dot(ys_ref[...], wo1_ref[...]) + _dot(yr_ref[...], wo2_ref[...])
    h = _rms(x, g_ref[...]).astype(BF16)
    q = _dot(h, wq_ref[...]).astype(BF16)
    outs = []
    for hd in range(X_HEADS):
        lanes = slice(hd * X_HEAD_DIM, (hd + 1) * X_HEAD_DIM)
        s = lax.dot_general(q[:, lanes], k_ref[:, lanes], (((1,), (1,)), ((), ())),
                            preferred_element_type=F32) * (X_HEAD_DIM ** -0.5)
        e = jnp.exp(s - jnp.max(s, axis=-1, keepdims=True))
        p = e * (1.0 / jnp.sum(e, axis=-1, keepdims=True))
        outs.append(_dot(p.astype(BF16), v_ref[:, lanes]).astype(BF16))
    o = jnp.concatenate(outs, axis=1)
    o_ref[...] = x + _dot(o, wco_ref[...])


def _mix_out(x2d, y_ssm, y_ret, w_out, g, wq, k_mem, v_mem, wco, tiles_per_batch):
    n = x2d.shape[0]
    row = lambda w: pl.BlockSpec((TM, w), lambda i: (i, 0))
    mem = pl.BlockSpec((MEM_LEN, D_MODEL), lambda i: (i // tiles_per_batch, 0))
    sq = _const_spec((D_MODEL, D_MODEL))
    half = _const_spec((S5_WIDTH, D_MODEL))
    return pl.pallas_call(
        _mix_out_kernel,
        grid=(n // TM,),
        in_specs=[row(D_MODEL), row(S5_WIDTH), row(RET_WIDTH), half, half,
                  _const_spec((1, D_MODEL)), sq, mem, mem, sq],
        out_specs=row(D_MODEL),
        out_shape=jax.ShapeDtypeStruct((n, D_MODEL), F32),
        compiler_params=_params("arbitrary"),
        name="mix_out_xattn",
    )(x2d, y_ssm, y_ret, w_out[:S5_WIDTH], w_out[S5_WIDTH:], g.reshape(1, D_MODEL), wq, k_mem, v_mem, wco)


def _ffn_kernel(final, x_ref, g_ref, wg_ref, wu_ref, wd_ref, gf_ref, o_ref):
    x = x_ref[...]
    h = _rms(x, g_ref[...]).astype(BF16)
    acc = x
    for lo, hi in FF_SPLITS:
        act = jax.nn.silu(_dot(h, wg_ref[:, lo:hi])) * _dot(h, wu_ref[:, lo:hi])
        acc = acc + _dot(act.astype(BF16), wd_ref[lo:hi, :])
    if final:
        acc = _rms(acc, gf_ref[...])
    o_ref[...] = acc


def _ffn(x2d, g, wg, wu, wd, g_final, final):
    n = x2d.shape[0]
    row = pl.BlockSpec((TM, D_MODEL), lambda i: (i, 0))
    return pl.pallas_call(
        functools.partial(_ffn_kernel, final),
        grid=(n // TM,),
        in_specs=[row, _const_spec((1, D_MODEL)), _const_spec((D_MODEL, D_FF)),
                  _const_spec((D_MODEL, D_FF)), _const_spec((D_FF, D_MODEL)), _const_spec((1, D_MODEL))],
        out_specs=row,
        out_shape=jax.ShapeDtypeStruct((n, D_MODEL), F32),
        compiler_params=_params("arbitrary"),
        name="ffn_final" if final else "ffn",
    )(x2d, g.reshape(1, D_MODEL), wg, wu, wd, g_final.reshape(1, D_MODEL))


def kernel(x, mem, positions, norm_mix, w_in, s5_lambda_re, s5_lambda_im, s5_log_step, s5_b_re, s5_b_im, s5_c_re, s5_c_im, s5_d, s5_w_glu, s5_b_glu, s5_out_norm, ret_out_norm, w_out, norm_cross, norm_mem, w_cq, w_ck, w_cv, w_co, norm_ffn, w_gate, w_up, w_down, norm_final):
    nb, seq, _ = x.shape
    n = nb * seq
    assert seq % RET_TR == 0 and seq % S5_TC == 0 and seq % TM == 0
    bf = lambda a: a.astype(BF16)

    cos2, sin2 = _rope_tables(positions)
    a_r, a_i, bb_r, bb_i = _s5_discretise(s5_lambda_re, s5_lambda_im, s5_log_step, s5_b_re, s5_b_im)
    a_r = a_r.reshape(DEPTH, S5_KT, 1, S5_SLAB)
    a_i = a_i.reshape(DEPTH, S5_KT, 1, S5_SLAB)
    wb = bf(jnp.concatenate([_block_diag_tiles(bb_r), _block_diag_tiles(bb_i)], axis=-1))
    wcr = bf(_block_diag_tiles(jnp.swapaxes(s5_c_re, -1, -2)))
    wci = bf(_block_diag_tiles(jnp.swapaxes(s5_c_im, -1, -2)))
    k_mem, v_mem = _mem_kv(mem.reshape(nb * MEM_LEN, D_MODEL), norm_mem, bf(w_ck), bf(w_cv))
    tables = _ret_tables()

    x2d = x.reshape(n, D_MODEL)
    for l in range(DEPTH):
        u, q, k, v, gate = _in_proj(x2d, norm_mix[l], bf(w_in[l]), cos2, sin2)
        y_ssm = _s5_mixer(u.reshape(nb, seq, S5_WIDTH), wb[l], a_r[l], a_i[l], wcr[l], wci[l],
                          s5_d[l].reshape(1, S5_WIDTH), bf(s5_w_glu[l]), s5_b_glu[l].reshape(1, S5_WIDTH),
                          s5_out_norm[l].reshape(1, S5_WIDTH))
        y_ret = _retention(q, k, v, gate, tables, ret_out_norm[l], nb, seq)
        x2d = _mix_out(x2d, y_ssm.reshape(n, S5_WIDTH), y_ret, bf(w_out[l]), norm_cross[l], bf(w_cq[l]),
                       k_mem[l], v_mem[l], bf(w_co[l]), seq // TM)
        x2d = _ffn(x2d, norm_ffn[l], bf(w_gate[l]), bf(w_up[l]), bf(w_down[l]), norm_final, l == DEPTH - 1)
    return x2d.reshape(nb, seq, D_MODEL)
```

```python
import functools
import math

import jax
import jax.numpy as jnp
from jax import lax
from jax.experimental import pallas as pl
from jax.experimental.pallas import tpu as pltpu

D_MODEL = 1024
DEPTH = 4
MEM_LEN = 256
S5_WIDTH = 512
S5_GROUP_CH = 16
S5_GROUPS = 32
S5_STATE = 64
RET_HEADS = 4
RET_DK = 128
RET_WIDTH = 512
IN_WIDTH = S5_WIDTH + 4 * RET_WIDTH
X_HEADS = 4
X_HEAD_DIM = 256
D_FF = 2816
ROPE_BASE = 10000.0
EPS = 1e-6

LANES = 128
S5_KT = S5_WIDTH // LANES
S5_GPT = LANES // S5_GROUP_CH
S5_SLAB = S5_GPT * S5_STATE
S5_R = 4
S5_TK = 128
S5_PITCH = S5_TK + 8
RET_CHUNK = 256
RET_TR = 1024
TM = 512
FF_SPLITS = ((0, 1536), (1536, D_FF))

BF16 = jnp.bfloat16
F32 = jnp.float32
VMEM_LIMIT = 48 * 1024 * 1024


def _dot(a, b):
    return jnp.dot(a, b, preferred_element_type=F32)


def _rms(x, g):
    ms = jnp.mean(x * x, axis=-1, keepdims=True)
    return x * lax.rsqrt(ms + EPS) * g


def _const_spec(shape):
    nd = len(shape)
    return pl.BlockSpec(shape, lambda *_: (0,) * nd, pipeline_mode=pl.Buffered(1))


def _layer_spec(l, shape):
    nd = len(shape)
    return pl.BlockSpec((None,) + tuple(shape), lambda *_: (l,) + (0,) * nd, pipeline_mode=pl.Buffered(1))


def _params(*sem):
    return pltpu.CompilerParams(dimension_semantics=sem, vmem_limit_bytes=VMEM_LIMIT)


def _rope_kernel(pos_ref, invf_ref, sign_ref, c_ref, s_ref):
    ang = pos_ref[...] * invf_ref[...]
    c_ref[...] = jnp.cos(ang)
    s_ref[...] = jnp.sin(ang) * sign_ref[...]


def _rope_tables(positions):
    n = positions.size
    half = RET_DK // 2
    inv_freq = 1.0 / (ROPE_BASE ** (jnp.arange(half, dtype=F32) / half))
    invf2 = jnp.concatenate([inv_freq, inv_freq]).reshape(1, RET_DK)
    sign = jnp.concatenate([-jnp.ones((half,), F32), jnp.ones((half,), F32)]).reshape(1, RET_DK)
    pos = positions.astype(F32).reshape(n, 1)
    t = 1024
    return pl.pallas_call(
        _rope_kernel,
        grid=(n // t,),
        in_specs=[pl.BlockSpec((t, 1), lambda i: (i, 0)),
                  pl.BlockSpec((1, RET_DK), lambda i: (0, 0)),
                  pl.BlockSpec((1, RET_DK), lambda i: (0, 0))],
        out_specs=[pl.BlockSpec((t, RET_DK), lambda i: (i, 0))] * 2,
        out_shape=[jax.ShapeDtypeStruct((n, RET_DK), F32)] * 2,
        compiler_params=_params("arbitrary"),
        name="rope_tables",
    )(pos, invf2, sign)


def _s5_disc_kernel(lr_ref, li_ref, ls_ref, brt_ref, bit_ref, cr_ref, ci_ref,
                    ar_ref, ai_ref, bpr_ref, bpi_ref, cpr_ref, cpi_ref, tk_ref):
    lr = lr_ref[0]
    li = li_ref[0]
    step = jnp.exp(ls_ref[0])
    mag = jnp.exp(lr * step)
    a_r = mag * jnp.cos(li * step)
    a_i = mag * jnp.sin(li * step)
    den = lr * lr + li * li
    f_r = ((a_r - 1.0) * lr + a_i * li) / den
    f_i = (a_i * lr - (a_r - 1.0) * li) / den
    br = brt_ref[0]
    bi = bit_ref[0]
    bb_r = f_r * br - f_i * bi
    bb_i = f_r * bi + f_i * br
    cr = cr_ref[0]
    ci = ci_ref[0]
    contract_p = (((2,), (2,)), ((0,), (0,)))
    pw_r = jnp.ones_like(a_r)
    pw_i = jnp.zeros_like(a_i)
    for m in range(S5_R):
        abb_r = pw_r * bb_r - pw_i * bb_i
        abb_i = pw_r * bb_i + pw_i * bb_r
        bpr_ref[0, S5_R - 1 - m] = abb_r
        bpi_ref[0, S5_R - 1 - m] = abb_i
        tk_ref[0, m] = (
            lax.dot_general(cr, abb_r, contract_p, precision=lax.Precision.HIGHEST, preferred_element_type=F32)
            - lax.dot_general(ci, abb_i, contract_p, precision=lax.Precision.HIGHEST, preferred_element_type=F32))
        pw_r, pw_i = pw_r * a_r - pw_i * a_i, pw_r * a_i + pw_i * a_r
        cpr_ref[0, m] = cr * pw_r - ci * pw_i
        cpi_ref[0, m] = -(cr * pw_i + ci * pw_r)
    ar_ref[0] = pw_r
    ai_ref[0] = pw_i


def _s5_discretise(lam_re, lam_im, log_step, b_re, b_im, c_re, c_im):
    g, p, h = S5_GROUPS, S5_STATE, S5_GROUP_CH
    lr = lam_re.reshape(DEPTH, g, 1, p)
    li = lam_im.reshape(DEPTH, g, 1, p)
    ls = log_step.reshape(DEPTH, g, 1, 1)
    brt = jnp.swapaxes(b_re, -1, -2)
    bit = jnp.swapaxes(b_im, -1, -2)
    vec = pl.BlockSpec((1, g, 1, p), lambda l: (l, 0, 0, 0))
    mat = pl.BlockSpec((1, g, h, p), lambda l: (l, 0, 0, 0))
    rmat = pl.BlockSpec((1, S5_R, g, h, p), lambda l: (l, 0, 0, 0, 0))
    rmat_shape = jax.ShapeDtypeStruct((DEPTH, S5_R, g, h, p), F32)
    return pl.pallas_call(
        _s5_disc_kernel,
        grid=(DEPTH,),
        in_specs=[vec, vec, pl.BlockSpec((1, g, 1, 1), lambda l: (l, 0, 0, 0)), mat, mat, mat, mat],
        out_specs=[vec, vec, rmat, rmat, rmat, rmat,
                   pl.BlockSpec((1, S5_R, g, h, h), lambda l: (l, 0, 0, 0, 0))],
        out_shape=[jax.ShapeDtypeStruct((DEPTH, g, 1, p), F32)] * 2 + [rmat_shape] * 4
        + [jax.ShapeDtypeStruct((DEPTH, S5_R, g, h, h), F32)],
        compiler_params=_params("arbitrary"),
        name="s5_discretise",
    )(lr, li, ls, brt, bit, c_re, c_im)


def _block_diag_tiles(blocks):
    lead = blocks.shape[:-3]
    r, c = blocks.shape[-2:]
    b = blocks.reshape(lead + (S5_KT, S5_GPT, r, c))
    eye = jnp.eye(S5_GPT, dtype=blocks.dtype)
    out = b[..., :, :, None, :] * eye[:, None, :, None]
    return out.reshape(lead + (S5_KT, S5_GPT * r, S5_GPT * c))


def _s5_weights(bp_r, bp_i, cp_r, cp_ni, tk):
    bp = jnp.concatenate([_block_diag_tiles(bp_r), _block_diag_tiles(bp_i)], axis=-1)
    bp = jnp.moveaxis(bp, 1, 2).reshape(DEPTH, S5_KT, S5_R * LANES, 2 * S5_SLAB)
    def readout(cp):
        t = _block_diag_tiles(jnp.swapaxes(cp, -1, -2))
        return jnp.moveaxis(t, 1, 3).reshape(DEPTH, S5_KT, S5_SLAB, S5_R * LANES)
    t = _block_diag_tiles(jnp.swapaxes(tk, -1, -2))
    zero = jnp.zeros_like(t[:, 0])
    toep = jnp.stack([jnp.stack([t[:, j - i] if j >= i else zero for j in range(S5_R)], axis=3)
                      for i in range(S5_R)], axis=2)
    toep = toep.reshape(DEPTH, S5_KT, S5_R * LANES, S5_R * LANES)
    w = jnp.concatenate([readout(cp_r), readout(cp_ni), toep], axis=2)
    return bp.astype(BF16), w.astype(BF16)


def _memkv_kernel(mem_ref, g_ref, wk_ref, wv_ref, k_ref, v_ref):
    m = _rms(mem_ref[...], g_ref[0]).astype(BF16)
    k_ref[0] = _dot(m, wk_ref[0]).astype(BF16)
    v_ref[0] = _dot(m, wv_ref[0]).astype(BF16)


def _mem_kv(mem2d, norm_mem, w_ck, w_cv):
    rows = mem2d.shape[0]
    wspec = pl.BlockSpec((1, D_MODEL, D_MODEL), lambda l: (l, 0, 0))
    ospec = pl.BlockSpec((1, rows, D_MODEL), lambda l: (l, 0, 0))
    return pl.pallas_call(
        _memkv_kernel,
        grid=(DEPTH,),
        in_specs=[pl.BlockSpec((rows, D_MODEL), lambda l: (0, 0)),
                  pl.BlockSpec((1, 1, D_MODEL), lambda l: (l, 0, 0)), wspec, wspec],
        out_specs=[ospec, ospec],
        out_shape=[jax.ShapeDtypeStruct((DEPTH, rows, D_MODEL), BF16)] * 2,
        compiler_params=_params("arbitrary"),
        name="mem_kv",
    )(mem2d, norm_mem.reshape(DEPTH, 1, D_MODEL), w_ck, w_cv)


def _in_proj_kernel(x_ref, g_ref, w_ref, c_ref, s_ref, u_ref, q_ref, k_ref, v_ref, gate_ref):
    h = _rms(x_ref[...], g_ref[...]).astype(BF16)
    cos = c_ref[...]
    sin = s_ref[...]
    u_ref[...] = _dot(h, w_ref[:, 0:S5_WIDTH])
    for dst, off, scale in ((q_ref, S5_WIDTH, None), (k_ref, S5_WIDTH + RET_WIDTH, RET_DK ** -0.5)):
        t = _dot(h, w_ref[:, off:off + RET_WIDTH])
        for hd in range(RET_HEADS):
            th = t[:, hd * RET_DK:(hd + 1) * RET_DK]
            r = th * cos + pltpu.roll(th, RET_DK // 2, axis=1) * sin
            if scale is not None:
                r = r * scale
            dst[:, hd * RET_DK:(hd + 1) * RET_DK] = r.astype(BF16)
    off = S5_WIDTH + 2 * RET_WIDTH
    v_ref[...] = _dot(h, w_ref[:, off:off + RET_WIDTH]).astype(BF16)
    gate_ref[...] = _dot(h, w_ref[:, off + RET_WIDTH:off + 2 * RET_WIDTH])


def _in_proj(l, x2d, g, w_bf16, cos2, sin2):
    n = x2d.shape[0]
    row = lambda w: pl.BlockSpec((TM, w), lambda i: (i, 0))
    return pl.pallas_call(
        _in_proj_kernel,
        grid=(n // TM,),
        in_specs=[row(D_MODEL), _layer_spec(l, (1, D_MODEL)), _layer_spec(l, (D_MODEL, IN_WIDTH)),
                  row(RET_DK), row(RET_DK)],
        out_specs=[row(S5_WIDTH), row(RET_WIDTH), row(RET_WIDTH), row(RET_WIDTH), row(RET_WIDTH)],
        out_shape=[jax.ShapeDtypeStruct((n, S5_WIDTH), F32),
                   jax.ShapeDtypeStruct((n, RET_WIDTH), BF16),
                   jax.ShapeDtypeStruct((n, RET_WIDTH), BF16),
                   jax.ShapeDtypeStruct((n, RET_WIDTH), BF16),
                   jax.ShapeDtypeStruct((n, RET_WIDTH), F32)],
        compiler_params=_params("arbitrary"),
        name="in_proj",
    )(x2d, g, w_bf16, cos2, sin2)


def _s5_kernel(nb, u_ref, bp_ref, ar_ref, ai_ref, w_ref, d_ref, wglu_ref, bglu_ref, gout_ref,
               o_ref, xs_ref, st_ref, y_ref):
    nslab = S5_SLAB // LANES
    rows = nb * S5_TK

    @pl.when(pl.program_id(0) == 0)
    def _():
        st_ref[...] = jnp.zeros_like(st_ref)

    for kt in range(S5_KT):
        def block_inputs():
            cols = [u_ref[:, :, i * S5_WIDTH + kt * LANES:i * S5_WIDTH + (kt + 1) * LANES] for i in range(S5_R)]
            return jnp.concatenate(cols, axis=-1).reshape(rows, S5_R * LANES).astype(BF16)

        xb = _dot(block_inputs(), bp_ref[kt])
        for b in range(nb):
            for j in range(2 * nslab):
                xs_ref[j, b * S5_PITCH:b * S5_PITCH + S5_TK, :] = (
                    xb[b * S5_TK:(b + 1) * S5_TK, j * LANES:(j + 1) * LANES])

        a_r = [jnp.broadcast_to(ar_ref[kt, :, j * LANES:(j + 1) * LANES], (nb, LANES)) for j in range(nslab)]
        a_i = [jnp.broadcast_to(ai_ref[kt, :, j * LANES:(j + 1) * LANES], (nb, LANES)) for j in range(nslab)]
        s_r0 = tuple(st_ref[kt, j] for j in range(nslab))
        s_i0 = tuple(st_ref[kt, nslab + j] for j in range(nslab))

        def step(t, carry):
            s_r, s_i = carry
            n_r, n_i = [], []
            for j in range(nslab):
                at = pl.ds(t, nb, stride=S5_PITCH)
                x_r = xs_ref[j, at, :]
                x_i = xs_ref[nslab + j, at, :]
                xs_ref[j, at, :] = s_r[j]
                xs_ref[nslab + j, at, :] = s_i[j]
                n_r.append(a_r[j] * s_r[j] - a_i[j] * s_i[j] + x_r)
                n_i.append(a_r[j] * s_i[j] + a_i[j] * s_r[j] + x_i)
            return tuple(n_r), tuple(n_i)

        s_r, s_i = lax.fori_loop(0, S5_TK, step, (s_r0, s_i0), unroll=8)
        for j in range(nslab):
            st_ref[kt, j] = s_r[j]
            st_ref[kt, nslab + j] = s_i[j]

        s_prev = jnp.concatenate(
            [jnp.concatenate([xs_ref[j, b * S5_PITCH:b * S5_PITCH + S5_TK, :] for j in range(2 * nslab)], axis=1)
             for b in range(nb)], axis=0).astype(BF16)
        yk = _dot(jnp.concatenate([s_prev, block_inputs()], axis=1), w_ref[kt])
        for j in range(S5_R):
            y_ref[:, :, j * S5_WIDTH + kt * LANES:j * S5_WIDTH + (kt + 1) * LANES] = (
                yk[:, j * LANES:(j + 1) * LANES].reshape(nb, S5_TK, LANES))

    for i in range(S5_R):
        cols = slice(i * S5_WIDTH, (i + 1) * S5_WIDTH)
        u = u_ref[:, :, cols].reshape(rows, S5_WIDTH)
        y = y_ref[:, :, cols].reshape(rows, S5_WIDTH) + d_ref[...] * u
        z = 0.5 * y * (1.0 + lax.erf(y * math.sqrt(0.5)))
        z = z * jax.nn.sigmoid(_dot(z.astype(BF16), wglu_ref[...]) + bglu_ref[...])
        o_ref[:, :, cols] = _rms(z, gout_ref[...]).astype(BF16).reshape(nb, S5_TK, S5_WIDTH)


def _s5_mixer(l, u3, bp, a_r, a_i, w, d, wglu, bglu, gout):
    nb, krows, width = u3.shape
    nslab2 = 2 * S5_SLAB // LANES
    tok = pl.BlockSpec((nb, S5_TK, width), lambda c: (0, c, 0))
    return pl.pallas_call(
        functools.partial(_s5_kernel, nb),
        grid=(krows // S5_TK,),
        in_specs=[tok,
                  _layer_spec(l, (S5_KT, S5_R * LANES, 2 * S5_SLAB)),
                  _layer_spec(l, (S5_KT, 1, S5_SLAB)), _layer_spec(l, (S5_KT, 1, S5_SLAB)),
                  _layer_spec(l, (S5_KT, 2 * S5_SLAB + S5_R * LANES, S5_R * LANES)),
                  _layer_spec(l, (1, S5_WIDTH)), _layer_spec(l, (S5_WIDTH, S5_WIDTH)),
                  _layer_spec(l, (1, S5_WIDTH)), _layer_spec(l, (1, S5_WIDTH))],
        out_specs=tok,
        out_shape=jax.ShapeDtypeStruct((nb, krows, width), BF16),
        scratch_shapes=[pltpu.VMEM((nslab2, nb * S5_PITCH, LANES), F32),
                        pltpu.VMEM((S5_KT, nslab2, nb, LANES), F32),
                        pltpu.VMEM((nb, S5_TK, width), F32)],
        compiler_params=_params("arbitrary"),
        name="s5_mixer",
    )(u3, bp, a_r, a_i, w, d, wglu, bglu, gout)


def _ret_kernel(q_ref, k_ref, v_ref, g_ref, lg_ref, gain_ref, o_ref, st_ref, dm_ref, zeta_ref, xi_ref):
    @pl.when((pl.program_id(0) == 0) & (pl.program_id(1) == 0))
    def _():
        row = lax.broadcasted_iota(jnp.int32, (RET_CHUNK, RET_CHUNK), 0).astype(F32)
        col = lax.broadcasted_iota(jnp.int32, (RET_CHUNK, RET_CHUNK), 1).astype(F32)
        diff = row - col
        idx = lax.broadcasted_iota(jnp.int32, (RET_CHUNK, RET_DK), 0).astype(F32)
        for hd in range(RET_HEADS):
            lg = lg_ref[hd]
            lg_wide = jnp.concatenate([lg] * (RET_CHUNK // RET_DK), axis=1)
            dm_ref[hd] = jnp.where(diff >= 0, jnp.exp(jnp.maximum(diff, 0.0) * lg_wide), 0.0)
            zeta_ref[hd] = jnp.exp((RET_CHUNK - 1.0 - idx) * lg)
            xi_ref[hd] = jnp.exp((idx + 1.0) * lg)

    @pl.when(pl.program_id(1) == 0)
    def _():
        st_ref[...] = jnp.zeros_like(st_ref)

    def chunk(c, carry):
        rows = pl.ds(pl.multiple_of(c * RET_CHUNK, RET_CHUNK), RET_CHUNK)
        for hd in range(RET_HEADS):
            lanes = slice(hd * RET_DK, (hd + 1) * RET_DK)
            qh = q_ref[rows, lanes]
            kh = k_ref[rows, lanes]
            vh = v_ref[rows, lanes]
            s = lax.dot_general(qh, kh, (((1,), (1,)), ((), ())), preferred_element_type=F32)
            s = s * dm_ref[hd]
            inner = _dot(s.astype(BF16), vh)
            state = st_ref[hd]
            cross = _dot(qh, state.astype(BF16)) * xi_ref[hd]
            kz = (kh.astype(F32) * zeta_ref[hd]).astype(BF16)
            kv = lax.dot_general(kz, vh, (((0,), (0,)), ((), ())), preferred_element_type=F32)
            st_ref[hd] = jnp.exp(RET_CHUNK * lg_ref[hd]) * state + kv
            o = inner + cross
            mu = jnp.mean(o, axis=-1, keepdims=True)
            oc = o - mu
            var = jnp.mean(oc * oc, axis=-1, keepdims=True)
            o = oc * lax.rsqrt(var + EPS) * gain_ref[hd]
            o = o * jax.nn.silu(g_ref[rows, lanes])
            o_ref[rows, lanes] = o.astype(BF16)
        return carry

    lax.fori_loop(0, RET_TR // RET_CHUNK, chunk, 0)


def _retention(l, q, k, v, gate, lg, gain, nb, seq):
    steps = seq // RET_TR
    tok = pl.BlockSpec((RET_TR, RET_WIDTH), lambda b, c: (b * steps + c, 0))
    return pl.pallas_call(
        _ret_kernel,
        grid=(nb, steps),
        in_specs=[tok, tok, tok, tok, _const_spec((RET_HEADS, 1, RET_DK)),
                  _layer_spec(l, (RET_HEADS, 1, RET_DK))],
        out_specs=tok,
        out_shape=jax.ShapeDtypeStruct((nb * seq, RET_WIDTH), BF16),
        scratch_shapes=[pltpu.VMEM((RET_HEADS, RET_DK, RET_DK), F32),
                        pltpu.VMEM((RET_HEADS, RET_CHUNK, RET_CHUNK), F32),
                        pltpu.VMEM((RET_HEADS, RET_CHUNK, RET_DK), F32),
                        pltpu.VMEM((RET_HEADS, RET_CHUNK, RET_DK), F32)],
        compiler_params=_params("arbitrary", "arbitrary"),
        name="retention",
    )(q, k, v, gate, lg, gain)


def _mix_out_kernel(x_ref, ys_ref, yr_ref, wo_ref, g_ref, wq_ref, k_ref, v_ref, wco_ref, o_ref):
    x = x_ref[...] + _dot(ys_ref[...], wo_ref[0:S5_WIDTH, :]) + _dot(yr_ref[...], wo_ref[S5_WIDTH:, :])
    h = _rms(x, g_ref[...]).astype(BF16)
    q = _dot(h, wq_ref[...]).astype(BF16)
    outs = []
    for hd in range(X_HEADS):
        lanes = slice(hd * X_HEAD_DIM, (hd + 1) * X_HEAD_DIM)
        s = lax.dot_general(q[:, lanes], k_ref[:, lanes], (((1,), (1,)), ((), ())),
                            preferred_element_type=F32) * (X_HEAD_DIM ** -0.5)
        e = jnp.exp(s - jnp.max(s, axis=-1, keepdims=True))
        p = e * (1.0 / jnp.sum(e, axis=-1, keepdims=True))
        outs.append(_dot(p.astype(BF16), v_ref[:, lanes]).astype(BF16))
    o = jnp.concatenate(outs, axis=1)
    o_ref[...] = x + _dot(o, wco_ref[...])


def _mix_out(l, x2d, y_ssm, y_ret, w_out, g, wq, k_mem, v_mem, wco, tiles_per_batch):
    n = x2d.shape[0]
    row = lambda w: pl.BlockSpec((TM, w), lambda i: (i, 0))
    mem = pl.BlockSpec((None, MEM_LEN, D_MODEL), lambda i: (l, i // tiles_per_batch, 0))
    sq = _layer_spec(l, (D_MODEL, D_MODEL))
    return pl.pallas_call(
        _mix_out_kernel,
        grid=(n // TM,),
        in_specs=[row(D_MODEL), row(S5_WIDTH), row(RET_WIDTH), sq,
                  _layer_spec(l, (1, D_MODEL)), sq, mem, mem, sq],
        out_specs=row(D_MODEL),
        out_shape=jax.ShapeDtypeStruct((n, D_MODEL), F32),
        compiler_params=_params("arbitrary"),
        name="mix_out_xattn",
    )(x2d, y_ssm, y_ret, w_out, g, wq, k_mem, v_mem, wco)


def _ffn_kernel(final, x_ref, g_ref, wg_ref, wu_ref, wd_ref, gf_ref, o_ref):
    x = x_ref[...]
    h = _rms(x, g_ref[...]).astype(BF16)
    acc = x
    for lo, hi in FF_SPLITS:
        act = jax.nn.silu(_dot(h, wg_ref[:, lo:hi])) * _dot(h, wu_ref[:, lo:hi])
        acc = acc + _dot(act.astype(BF16), wd_ref[lo:hi, :])
    if final:
        acc = _rms(acc, gf_ref[...])
    o_ref[...] = acc


def _ffn(l, x2d, g, wg, wu, wd, g_final, final):
    n = x2d.shape[0]
    row = pl.BlockSpec((TM, D_MODEL), lambda i: (i, 0))
    return pl.pallas_call(
        functools.partial(_ffn_kernel, final),
        grid=(n // TM,),
        in_specs=[row, _layer_spec(l, (1, D_MODEL)), _layer_spec(l, (D_MODEL, D_FF)),
                  _layer_spec(l, (D_MODEL, D_FF)), _layer_spec(l, (D_FF, D_MODEL)), _const_spec((1, D_MODEL))],
        out_specs=row,
        out_shape=jax.ShapeDtypeStruct((n, D_MODEL), F32),
        compiler_params=_params("arbitrary"),
        name="ffn_final" if final else "ffn",
    )(x2d, g, wg, wu, wd, g_final.reshape(1, D_MODEL))


def kernel(x, mem, positions, norm_mix, w_in, s5_lambda_re, s5_lambda_im, s5_log_step, s5_b_re, s5_b_im, s5_c_re, s5_c_im, s5_d, s5_w_glu, s5_b_glu, s5_out_norm, ret_out_norm, w_out, norm_cross, norm_mem, w_cq, w_ck, w_cv, w_co, norm_ffn, w_gate, w_up, w_down, norm_final):
    nb, seq, _ = x.shape
    n = nb * seq
    assert seq % RET_TR == 0 and seq % (S5_R * S5_TK) == 0 and seq % TM == 0
    bf = lambda a: a.astype(BF16)
    vec = lambda a: a.reshape(DEPTH, 1, a.shape[-1])

    cos2, sin2 = _rope_tables(positions)
    a_r, a_i, bp_r, bp_i, cp_r, cp_ni, tk = _s5_discretise(
        s5_lambda_re, s5_lambda_im, s5_log_step, s5_b_re, s5_b_im, s5_c_re, s5_c_im)
    a_r = a_r.reshape(DEPTH, S5_KT, 1, S5_SLAB)
    a_i = a_i.reshape(DEPTH, S5_KT, 1, S5_SLAB)
    s5_bp, s5_w = _s5_weights(bp_r, bp_i, cp_r, cp_ni, tk)
    k_mem, v_mem = _mem_kv(mem.reshape(nb * MEM_LEN, D_MODEL), norm_mem, bf(w_ck), bf(w_cv))
    lg = jnp.log1p(-jnp.exp2(-5.0 - jnp.arange(RET_HEADS, dtype=F32)))
    lg = jnp.broadcast_to(lg[:, None, None], (RET_HEADS, 1, RET_DK))
    ret_gain = ret_out_norm.reshape(DEPTH, RET_HEADS, 1, RET_DK)
    w_in_b, w_out_b, w_cq_b, w_co_b = bf(w_in), bf(w_out), bf(w_cq), bf(w_co)
    w_gate_b, w_up_b, w_down_b, w_glu_b = bf(w_gate), bf(w_up), bf(w_down), bf(s5_w_glu)
    g_mix, g_cross, g_ffn = vec(norm_mix), vec(norm_cross), vec(norm_ffn)
    s5_d3, s5_bglu3, s5_gout3 = vec(s5_d), vec(s5_b_glu), vec(s5_out_norm)

    x2d = x.reshape(n, D_MODEL)
    for l in range(DEPTH):
        u, q, k, v, gate = _in_proj(l, x2d, g_mix, w_in_b, cos2, sin2)
        y_ssm = _s5_mixer(l, u.reshape(nb, seq // S5_R, S5_R * S5_WIDTH), s5_bp, a_r, a_i, s5_w,
                          s5_d3, w_glu_b, s5_bglu3, s5_gout3)
        y_ret = _retention(l, q, k, v, gate, lg, ret_gain, nb, seq)
        x2d = _mix_out(l, x2d, y_ssm.reshape(n, S5_WIDTH), y_ret, w_out_b, g_cross, w_cq_b,
                       k_mem, v_mem, w_co_b, seq // TM)
        x2d = _ffn(l, x2d, g_ffn, w_gate_b, w_up_b, w_down_b, norm_final, l == DEPTH - 1)
    return x2d.reshape(nb, seq, D_MODEL)
```

```python
import functools
import math

import jax
import jax.numpy as jnp
from jax import lax
from jax.experimental import pallas as pl
from jax.experimental.pallas import tpu as pltpu

D_MODEL = 1024
DEPTH = 4
MEM_LEN = 256
S5_WIDTH = 512
S5_GROUP_CH = 16
S5_GROUPS = 32
S5_STATE = 64
RET_HEADS = 4
RET_DK = 128
RET_WIDTH = 512
IN_WIDTH = S5_WIDTH + 4 * RET_WIDTH
X_HEADS = 4
X_HEAD_DIM = 256
D_FF = 2816
ROPE_BASE = 10000.0
EPS = 1e-6

LANES = 128
S5_KT = S5_WIDTH // LANES
S5_GPT = LANES // S5_GROUP_CH
S5_SLAB = S5_GPT * S5_STATE
S5_R = 4
S5_TK = 128
S5_PITCH = S5_TK + 8
RET_CHUNK = 256
RET_TR = 1024
TM = 1024
FF_SPLITS = ((0, 1536), (1536, D_FF))

BF16 = jnp.bfloat16
F32 = jnp.float32
VMEM_LIMIT = 48 * 1024 * 1024


def _dot(a, b):
    return jnp.dot(a, b, preferred_element_type=F32)


def _rms(x, g):
    ms = jnp.mean(x * x, axis=-1, keepdims=True)
    return x * lax.rsqrt(ms + EPS) * g


def _const_spec(shape):
    nd = len(shape)
    return pl.BlockSpec(shape, lambda *_: (0,) * nd, pipeline_mode=pl.Buffered(1))


def _layer_spec(l, shape):
    nd = len(shape)
    return pl.BlockSpec((None,) + tuple(shape), lambda *_: (l,) + (0,) * nd, pipeline_mode=pl.Buffered(1))


def _params(*sem):
    return pltpu.CompilerParams(dimension_semantics=sem, vmem_limit_bytes=VMEM_LIMIT)


def _rope_kernel(pos_ref, invf_ref, sign_ref, c_ref, s_ref):
    ang = pos_ref[...] * invf_ref[...]
    c_ref[...] = jnp.cos(ang)
    s_ref[...] = jnp.sin(ang) * sign_ref[...]


def _rope_tables(positions):
    n = positions.size
    half = RET_DK // 2
    inv_freq = 1.0 / (ROPE_BASE ** (jnp.arange(half, dtype=F32) / half))
    invf2 = jnp.concatenate([inv_freq, inv_freq]).reshape(1, RET_DK)
    sign = jnp.concatenate([-jnp.ones((half,), F32), jnp.ones((half,), F32)]).reshape(1, RET_DK)
    pos = positions.astype(F32).reshape(n, 1)
    t = 1024
    return pl.pallas_call(
        _rope_kernel,
        grid=(n // t,),
        in_specs=[pl.BlockSpec((t, 1), lambda i: (i, 0)),
                  pl.BlockSpec((1, RET_DK), lambda i: (0, 0)),
                  pl.BlockSpec((1, RET_DK), lambda i: (0, 0))],
        out_specs=[pl.BlockSpec((t, RET_DK), lambda i: (i, 0))] * 2,
        out_shape=[jax.ShapeDtypeStruct((n, RET_DK), F32)] * 2,
        compiler_params=_params("arbitrary"),
        name="rope_tables",
    )(pos, invf2, sign)


def _s5_disc_kernel(lr_ref, li_ref, ls_ref, brt_ref, bit_ref, cr_ref, ci_ref,
                    ar_ref, ai_ref, bpr_ref, bpi_ref, cpr_ref, cpi_ref, tk_ref):
    lr = lr_ref[0]
    li = li_ref[0]
    step = jnp.exp(ls_ref[0])
    mag = jnp.exp(lr * step)
    a_r = mag * jnp.cos(li * step)
    a_i = mag * jnp.sin(li * step)
    den = lr * lr + li * li
    f_r = ((a_r - 1.0) * lr + a_i * li) / den
    f_i = (a_i * lr - (a_r - 1.0) * li) / den
    br = brt_ref[0]
    bi = bit_ref[0]
    bb_r = f_r * br - f_i * bi
    bb_i = f_r * bi + f_i * br
    cr = cr_ref[0]
    ci = ci_ref[0]
    contract_p = (((2,), (2,)), ((0,), (0,)))
    pw_r = jnp.ones_like(a_r)
    pw_i = jnp.zeros_like(a_i)
    for m in range(S5_R):
        abb_r = pw_r * bb_r - pw_i * bb_i
        abb_i = pw_r * bb_i + pw_i * bb_r
        bpr_ref[0, S5_R - 1 - m] = abb_r
        bpi_ref[0, S5_R - 1 - m] = abb_i
        tk_ref[0, m] = (
            lax.dot_general(cr, abb_r, contract_p, precision=lax.Precision.HIGHEST, preferred_element_type=F32)
            - lax.dot_general(ci, abb_i, contract_p, precision=lax.Precision.HIGHEST, preferred_element_type=F32))
        pw_r, pw_i = pw_r * a_r - pw_i * a_i, pw_r * a_i + pw_i * a_r
        cpr_ref[0, m] = cr * pw_r - ci * pw_i
        cpi_ref[0, m] = -(cr * pw_i + ci * pw_r)
    ar_ref[0] = pw_r
    ai_ref[0] = pw_i


def _s5_discretise(lam_re, lam_im, log_step, b_re, b_im, c_re, c_im):
    g, p, h = S5_GROUPS, S5_STATE, S5_GROUP_CH
    lr = lam_re.reshape(DEPTH, g, 1, p)
    li = lam_im.reshape(DEPTH, g, 1, p)
    ls = log_step.reshape(DEPTH, g, 1, 1)
    brt = jnp.swapaxes(b_re, -1, -2)
    bit = jnp.swapaxes(b_im, -1, -2)
    vec = pl.BlockSpec((1, g, 1, p), lambda l: (l, 0, 0, 0))
    mat = pl.BlockSpec((1, g, h, p), lambda l: (l, 0, 0, 0))
    rmat = pl.BlockSpec((1, S5_R, g, h, p), lambda l: (l, 0, 0, 0, 0))
    rmat_shape = jax.ShapeDtypeStruct((DEPTH, S5_R, g, h, p), F32)
    return pl.pallas_call(
        _s5_disc_kernel,
        grid=(DEPTH,),
        in_specs=[vec, vec, pl.BlockSpec((1, g, 1, 1), lambda l: (l, 0, 0, 0)), mat, mat, mat, mat],
        out_specs=[vec, vec, rmat, rmat, rmat, rmat,
                   pl.BlockSpec((1, S5_R, g, h, h), lambda l: (l, 0, 0, 0, 0))],
        out_shape=[jax.ShapeDtypeStruct((DEPTH, g, 1, p), F32)] * 2 + [rmat_shape] * 4
        + [jax.ShapeDtypeStruct((DEPTH, S5_R, g, h, h), F32)],
        compiler_params=_params("arbitrary"),
        name="s5_discretise",
    )(lr, li, ls, brt, bit, c_re, c_im)


def _block_diag_tiles(blocks):
    lead = blocks.shape[:-3]
    r, c = blocks.shape[-2:]
    b = blocks.reshape(lead + (S5_KT, S5_GPT, r, c))
    eye = jnp.eye(S5_GPT, dtype=blocks.dtype)
    out = b[..., :, :, None, :] * eye[:, None, :, None]
    return out.reshape(lead + (S5_KT, S5_GPT * r, S5_GPT * c))


def _s5_weights(bp_r, bp_i, cp_r, cp_ni, tk):
    bp_r, bp_i, cp_r, cp_ni, tk = (a.astype(BF16) for a in (bp_r, bp_i, cp_r, cp_ni, tk))
    bp = jnp.concatenate([_block_diag_tiles(bp_r), _block_diag_tiles(bp_i)], axis=-1)
    bp = jnp.moveaxis(bp, 1, 2).reshape(DEPTH, S5_KT, S5_R * LANES, 2 * S5_SLAB)
    def readout(cp):
        t = _block_diag_tiles(jnp.swapaxes(cp, -1, -2))
        return jnp.moveaxis(t, 1, 3).reshape(DEPTH, S5_KT, S5_SLAB, S5_R * LANES)
    t = _block_diag_tiles(jnp.swapaxes(tk, -1, -2))
    zero = jnp.zeros_like(t[:, 0])
    toep = jnp.stack([jnp.stack([t[:, j - i] if j >= i else zero for j in range(S5_R)], axis=3)
                      for i in range(S5_R)], axis=2)
    toep = toep.reshape(DEPTH, S5_KT, S5_R * LANES, S5_R * LANES)
    w = jnp.concatenate([readout(cp_r), readout(cp_ni), toep], axis=2)
    return bp, w


def _memkv_kernel(mem_ref, g_ref, wk_ref, wv_ref, k_ref, v_ref):
    m = _rms(mem_ref[...], g_ref[0]).astype(BF16)
    k_ref[0] = _dot(m, wk_ref[0]).astype(BF16)
    v_ref[0] = _dot(m, wv_ref[0]).astype(BF16)


def _mem_kv(mem2d, norm_mem, w_ck, w_cv):
    rows = mem2d.shape[0]
    wspec = pl.BlockSpec((1, D_MODEL, D_MODEL), lambda l: (l, 0, 0))
    ospec = pl.BlockSpec((1, rows, D_MODEL), lambda l: (l, 0, 0))
    return pl.pallas_call(
        _memkv_kernel,
        grid=(DEPTH,),
        in_specs=[pl.BlockSpec((rows, D_MODEL), lambda l: (0, 0)),
                  pl.BlockSpec((1, 1, D_MODEL), lambda l: (l, 0, 0)), wspec, wspec],
        out_specs=[ospec, ospec],
        out_shape=[jax.ShapeDtypeStruct((DEPTH, rows, D_MODEL), BF16)] * 2,
        compiler_params=_params("arbitrary"),
        name="mem_kv",
    )(mem2d, norm_mem.reshape(DEPTH, 1, D_MODEL), w_ck, w_cv)


def _in_proj_kernel(x_ref, g_ref, w_ref, c_ref, s_ref, u_ref, q_ref, k_ref, v_ref, gate_ref, us_ref):
    h = _rms(x_ref[...], g_ref[...]).astype(BF16)
    cos = c_ref[...]
    sin = s_ref[...]
    u = _dot(h, w_ref[:, 0:S5_WIDTH])
    for c in range(S5_KT):
        us_ref[c] = u[:, c * LANES:(c + 1) * LANES]
    for i in range(S5_R):
        for c in range(S5_KT):
            u_ref[:, i * S5_WIDTH + c * LANES:i * S5_WIDTH + (c + 1) * LANES] = (
                us_ref[c, pl.ds(i, TM // S5_R, stride=S5_R), :])
    for dst, off, scale in ((q_ref, S5_WIDTH, None), (k_ref, S5_WIDTH + RET_WIDTH, RET_DK ** -0.5)):
        t = _dot(h, w_ref[:, off:off + RET_WIDTH])
        for hd in range(RET_HEADS):
            th = t[:, hd * RET_DK:(hd + 1) * RET_DK]
            r = th * cos + pltpu.roll(th, RET_DK // 2, axis=1) * sin
            if scale is not None:
                r = r * scale
            dst[:, hd * RET_DK:(hd + 1) * RET_DK] = r.astype(BF16)
    off = S5_WIDTH + 2 * RET_WIDTH
    v_ref[...] = _dot(h, w_ref[:, off:off + RET_WIDTH]).astype(BF16)
    gate_ref[...] = _dot(h, w_ref[:, off + RET_WIDTH:off + 2 * RET_WIDTH])


def _in_proj(l, x2d, g, w_bf16, cos2, sin2):
    n = x2d.shape[0]
    row = lambda w: pl.BlockSpec((TM, w), lambda i: (i, 0))
    return pl.pallas_call(
        _in_proj_kernel,
        grid=(n // TM,),
        in_specs=[row(D_MODEL), _layer_spec(l, (1, D_MODEL)), _layer_spec(l, (D_MODEL, IN_WIDTH)),
                  row(RET_DK), row(RET_DK)],
        out_specs=[pl.BlockSpec((TM // S5_R, S5_R * S5_WIDTH), lambda i: (i, 0)),
                   row(RET_WIDTH), row(RET_WIDTH), row(RET_WIDTH), row(RET_WIDTH)],
        out_shape=[jax.ShapeDtypeStruct((n // S5_R, S5_R * S5_WIDTH), F32),
                   jax.ShapeDtypeStruct((n, RET_WIDTH), BF16),
                   jax.ShapeDtypeStruct((n, RET_WIDTH), BF16),
                   jax.ShapeDtypeStruct((n, RET_WIDTH), BF16),
                   jax.ShapeDtypeStruct((n, RET_WIDTH), F32)],
        scratch_shapes=[pltpu.VMEM((S5_KT, TM, LANES), F32)],
        compiler_params=_params("arbitrary"),
        name="in_proj",
    )(x2d, g, w_bf16, cos2, sin2)


def _s5_kernel(nb, u_ref, bp_ref, ar_ref, ai_ref, w_ref, d_ref, wglu_ref, bglu_ref, gout_ref,
               o_ref, xs_ref, st_ref, y_ref):
    nslab = S5_SLAB // LANES
    rows = nb * S5_TK

    @pl.when(pl.program_id(0) == 0)
    def _():
        st_ref[...] = jnp.zeros_like(st_ref)

    for kt in range(S5_KT):
        def block_inputs():
            cols = [u_ref[:, :, i * S5_WIDTH + kt * LANES:i * S5_WIDTH + (kt + 1) * LANES] for i in range(S5_R)]
            return jnp.concatenate(cols, axis=-1).reshape(rows, S5_R * LANES).astype(BF16)

        xb = _dot(block_inputs(), bp_ref[kt])
        for b in range(nb):
            for j in range(2 * nslab):
                xs_ref[j, b * S5_PITCH:b * S5_PITCH + S5_TK, :] = (
                    xb[b * S5_TK:(b + 1) * S5_TK, j * LANES:(j + 1) * LANES])

        a_r = [jnp.broadcast_to(ar_ref[kt, :, j * LANES:(j + 1) * LANES], (nb, LANES)) for j in range(nslab)]
        a_i = [jnp.broadcast_to(ai_ref[kt, :, j * LANES:(j + 1) * LANES], (nb, LANES)) for j in range(nslab)]
        s_r0 = tuple(st_ref[kt, j] for j in range(nslab))
        s_i0 = tuple(st_ref[kt, nslab + j] for j in range(nslab))

        def step(t, carry):
            s_r, s_i = carry
            n_r, n_i = [], []
            for j in range(nslab):
                at = pl.ds(t, nb, stride=S5_PITCH)
                x_r = xs_ref[j, at, :]
                x_i = xs_ref[nslab + j, at, :]
                xs_ref[j, at, :] = s_r[j]
                xs_ref[nslab + j, at, :] = s_i[j]
                n_r.append(a_r[j] * s_r[j] - a_i[j] * s_i[j] + x_r)
                n_i.append(a_r[j] * s_i[j] + a_i[j] * s_r[j] + x_i)
            return tuple(n_r), tuple(n_i)

        s_r, s_i = lax.fori_loop(0, S5_TK, step, (s_r0, s_i0), unroll=8)
        for j in range(nslab):
            st_ref[kt, j] = s_r[j]
            st_ref[kt, nslab + j] = s_i[j]

        s_prev = jnp.concatenate(
            [jnp.concatenate([xs_ref[j, b * S5_PITCH:b * S5_PITCH + S5_TK, :] for j in range(2 * nslab)], axis=1)
             for b in range(nb)], axis=0).astype(BF16)
        yk = _dot(jnp.concatenate([s_prev, block_inputs()], axis=1), w_ref[kt])
        for j in range(S5_R):
            y_ref[:, :, j * S5_WIDTH + kt * LANES:j * S5_WIDTH + (kt + 1) * LANES] = (
                yk[:, j * LANES:(j + 1) * LANES].reshape(nb, S5_TK, LANES))

    for i in range(S5_R):
        cols = slice(i * S5_WIDTH, (i + 1) * S5_WIDTH)
        u = u_ref[:, :, cols].reshape(rows, S5_WIDTH)
        y = y_ref[:, :, cols].reshape(rows, S5_WIDTH) + d_ref[...] * u
        z = 0.5 * y * (1.0 + lax.erf(y * math.sqrt(0.5)))
        z = z * jax.nn.sigmoid(_dot(z.astype(BF16), wglu_ref[...]) + bglu_ref[...])
        o_ref[:, :, cols] = _rms(z, gout_ref[...]).astype(BF16).reshape(nb, S5_TK, S5_WIDTH)


def _s5_mixer(l, u3, bp, a_r, a_i, w, d, wglu, bglu, gout):
    nb, krows, width = u3.shape
    nslab2 = 2 * S5_SLAB // LANES
    tok = pl.BlockSpec((nb, S5_TK, width), lambda c: (0, c, 0))
    return pl.pallas_call(
        functools.partial(_s5_kernel, nb),
        grid=(krows // S5_TK,),
        in_specs=[tok,
                  _layer_spec(l, (S5_KT, S5_R * LANES, 2 * S5_SLAB)),
                  _layer_spec(l, (S5_KT, 1, S5_SLAB)), _layer_spec(l, (S5_KT, 1, S5_SLAB)),
                  _layer_spec(l, (S5_KT, 2 * S5_SLAB + S5_R * LANES, S5_R * LANES)),
                  _layer_spec(l, (1, S5_WIDTH)), _layer_spec(l, (S5_WIDTH, S5_WIDTH)),
                  _layer_spec(l, (1, S5_WIDTH)), _layer_spec(l, (1, S5_WIDTH))],
        out_specs=tok,
        out_shape=jax.ShapeDtypeStruct((nb, krows, width), BF16),
        scratch_shapes=[pltpu.VMEM((nslab2, nb * S5_PITCH, LANES), F32),
                        pltpu.VMEM((S5_KT, nslab2, nb, LANES), F32),
                        pltpu.VMEM((nb, S5_TK, width), F32)],
        compiler_params=_params("arbitrary"),
        name="s5_mixer",
    )(u3, bp, a_r, a_i, w, d, wglu, bglu, gout)


def _ret_kernel(q_ref, k_ref, v_ref, g_ref, lg_ref, gain_ref, o_ref, st_ref, dm_ref, zeta_ref, xi_ref):
    @pl.when((pl.program_id(0) == 0) & (pl.program_id(1) == 0))
    def _():
        row = lax.broadcasted_iota(jnp.int32, (RET_CHUNK, RET_CHUNK), 0).astype(F32)
        col = lax.broadcasted_iota(jnp.int32, (RET_CHUNK, RET_CHUNK), 1).astype(F32)
        diff = row - col
        idx = lax.broadcasted_iota(jnp.int32, (RET_CHUNK, RET_DK), 0).astype(F32)
        for hd in range(RET_HEADS):
            lg = lg_ref[hd]
            lg_wide = jnp.concatenate([lg] * (RET_CHUNK // RET_DK), axis=1)
            dm_ref[hd] = jnp.where(diff >= 0, jnp.exp(jnp.maximum(diff, 0.0) * lg_wide), 0.0)
            zeta_ref[hd] = jnp.exp((RET_CHUNK - 1.0 - idx) * lg)
            xi_ref[hd] = jnp.exp((idx + 1.0) * lg)

    @pl.when(pl.program_id(1) == 0)
    def _():
        st_ref[...] = jnp.zeros_like(st_ref)

    def chunk(c, carry):
        rows = pl.ds(pl.multiple_of(c * RET_CHUNK, RET_CHUNK), RET_CHUNK)
        for hd in range(RET_HEADS):
            lanes = slice(hd * RET_DK, (hd + 1) * RET_DK)
            qh = q_ref[rows, lanes]
            kh = k_ref[rows, lanes]
            vh = v_ref[rows, lanes]
            s = lax.dot_general(qh, kh, (((1,), (1,)), ((), ())), preferred_element_type=F32)
            s = s * dm_ref[hd]
            inner = _dot(s.astype(BF16), vh)
            state = st_ref[hd]
            cross = _dot(qh, state.astype(BF16)) * xi_ref[hd]
            kz = (kh.astype(F32) * zeta_ref[hd]).astype(BF16)
            kv = lax.dot_general(kz, vh, (((0,), (0,)), ((), ())), preferred_element_type=F32)
            st_ref[hd] = jnp.exp(RET_CHUNK * lg_ref[hd]) * state + kv
            o = inner + cross
            mu = jnp.mean(o, axis=-1, keepdims=True)
            oc = o - mu
            var = jnp.mean(oc * oc, axis=-1, keepdims=True)
            o = oc * lax.rsqrt(var + EPS) * gain_ref[hd]
            o = o * jax.nn.silu(g_ref[rows, lanes])
            o_ref[rows, lanes] = o.astype(BF16)
        return carry

    lax.fori_loop(0, RET_TR // RET_CHUNK, chunk, 0)


def _retention(l, q, k, v, gate, lg, gain, nb, seq):
    steps = seq // RET_TR
    tok = pl.BlockSpec((RET_TR, RET_WIDTH), lambda b, c: (b * steps + c, 0))
    return pl.pallas_call(
        _ret_kernel,
        grid=(nb, steps),
        in_specs=[tok, tok, tok, tok, _const_spec((RET_HEADS, 1, RET_DK)),
                  _layer_spec(l, (RET_HEADS, 1, RET_DK))],
        out_specs=tok,
        out_shape=jax.ShapeDtypeStruct((nb * seq, RET_WIDTH), BF16),
        scratch_shapes=[pltpu.VMEM((RET_HEADS, RET_DK, RET_DK), F32),
                        pltpu.VMEM((RET_HEADS, RET_CHUNK, RET_CHUNK), F32),
                        pltpu.VMEM((RET_HEADS, RET_CHUNK, RET_DK), F32),
                        pltpu.VMEM((RET_HEADS, RET_CHUNK, RET_DK), F32)],
        compiler_params=_params("arbitrary", "arbitrary"),
        name="retention",
    )(q, k, v, gate, lg, gain)


def _mix_out_kernel(x_ref, ys_ref, yr_ref, wo_ref, g_ref, wq_ref, k_ref, v_ref, wco_ref, o_ref, ps_ref):
    tk = TM // S5_R
    ys = jnp.concatenate([ys_ref[:, i * S5_WIDTH:(i + 1) * S5_WIDTH] for i in range(S5_R)], axis=0)
    ps = _dot(ys, wo_ref[0:S5_WIDTH, :])
    for i in range(S5_R):
        for c in range(D_MODEL // LANES):
            ps_ref[c, pl.ds(i, tk, stride=S5_R), :] = ps[i * tk:(i + 1) * tk, c * LANES:(c + 1) * LANES]
    ssm = jnp.concatenate([ps_ref[c] for c in range(D_MODEL // LANES)], axis=1)
    x = x_ref[...] + ssm + _dot(yr_ref[...], wo_ref[S5_WIDTH:, :])
    h = _rms(x, g_ref[...]).astype(BF16)
    q = _dot(h, wq_ref[...]).astype(BF16)
    outs = []
    for hd in range(X_HEADS):
        lanes = slice(hd * X_HEAD_DIM, (hd + 1) * X_HEAD_DIM)
        s = lax.dot_general(q[:, lanes], k_ref[:, lanes], (((1,), (1,)), ((), ())),
                            preferred_element_type=F32) * (X_HEAD_DIM ** -0.5)
        e = jnp.exp(s - jnp.max(s, axis=-1, keepdims=True))
        p = e * (1.0 / jnp.sum(e, axis=-1, keepdims=True))
        outs.append(_dot(p.astype(BF16), v_ref[:, lanes]).astype(BF16))
    o = jnp.concatenate(outs, axis=1)
    o_ref[...] = x + _dot(o, wco_ref[...])


def _mix_out(l, x2d, y_ssm, y_ret, w_out, g, wq, k_mem, v_mem, wco, tiles_per_batch):
    n = x2d.shape[0]
    row = lambda w: pl.BlockSpec((TM, w), lambda i: (i, 0))
    mem = pl.BlockSpec((None, MEM_LEN, D_MODEL), lambda i: (l, i // tiles_per_batch, 0))
    sq = _layer_spec(l, (D_MODEL, D_MODEL))
    return pl.pallas_call(
        _mix_out_kernel,
        grid=(n // TM,),
        in_specs=[row(D_MODEL), pl.BlockSpec((TM // S5_R, S5_R * S5_WIDTH), lambda i: (i, 0)),
                  row(RET_WIDTH), sq, _layer_spec(l, (1, D_MODEL)), sq, mem, mem, sq],
        out_specs=row(D_MODEL),
        out_shape=jax.ShapeDtypeStruct((n, D_MODEL), F32),
        scratch_shapes=[pltpu.VMEM((D_MODEL // LANES, TM, LANES), F32)],
        compiler_params=_params("arbitrary"),
        name="mix_out_xattn",
    )(x2d, y_ssm, y_ret, w_out, g, wq, k_mem, v_mem, wco)


def _ffn_kernel(final, x_ref, g_ref, wg_ref, wu_ref, wd_ref, gf_ref, o_ref):
    x = x_ref[...]
    h = _rms(x, g_ref[...]).astype(BF16)
    acc = x
    for lo, hi in FF_SPLITS:
        act = jax.nn.silu(_dot(h, wg_ref[:, lo:hi])) * _dot(h, wu_ref[:, lo:hi])
        acc = acc + _dot(act.astype(BF16), wd_ref[lo:hi, :])
    if final:
        acc = _rms(acc, gf_ref[...])
    o_ref[...] = acc


def _ffn(l, x2d, g, wg, wu, wd, g_final, final):
    n = x2d.shape[0]
    row = pl.BlockSpec((TM, D_MODEL), lambda i: (i, 0))
    return pl.pallas_call(
        functools.partial(_ffn_kernel, final),
        grid=(n // TM,),
        in_specs=[row, _layer_spec(l, (1, D_MODEL)), _layer_spec(l, (D_MODEL, D_FF)),
                  _layer_spec(l, (D_MODEL, D_FF)), _layer_spec(l, (D_FF, D_MODEL)), _const_spec((1, D_MODEL))],
        out_specs=row,
        out_shape=jax.ShapeDtypeStruct((n, D_MODEL), F32),
        compiler_params=_params("arbitrary"),
        name="ffn_final" if final else "ffn",
    )(x2d, g, wg, wu, wd, g_final.reshape(1, D_MODEL))


def kernel(x, mem, positions, norm_mix, w_in, s5_lambda_re, s5_lambda_im, s5_log_step, s5_b_re, s5_b_im, s5_c_re, s5_c_im, s5_d, s5_w_glu, s5_b_glu, s5_out_norm, ret_out_norm, w_out, norm_cross, norm_mem, w_cq, w_ck, w_cv, w_co, norm_ffn, w_gate, w_up, w_down, norm_final):
    nb, seq, _ = x.shape
    n = nb * seq
    assert seq % RET_TR == 0 and seq % (S5_R * S5_TK) == 0 and seq % TM == 0
    bf = lambda a: a.astype(BF16)
    vec = lambda a: a.reshape(DEPTH, 1, a.shape[-1])

    cos2, sin2 = _rope_tables(positions)
    a_r, a_i, bp_r, bp_i, cp_r, cp_ni, tk = _s5_discretise(
        s5_lambda_re, s5_lambda_im, s5_log_step, s5_b_re, s5_b_im, s5_c_re, s5_c_im)
    a_r = a_r.reshape(DEPTH, S5_KT, 1, S5_SLAB)
    a_i = a_i.reshape(DEPTH, S5_KT, 1, S5_SLAB)
    s5_bp, s5_w = _s5_weights(bp_r, bp_i, cp_r, cp_ni, tk)
    k_mem, v_mem = _mem_kv(mem.reshape(nb * MEM_LEN, D_MODEL), norm_mem, bf(w_ck), bf(w_cv))
    lg = jnp.log1p(-jnp.exp2(-5.0 - jnp.arange(RET_HEADS, dtype=F32)))
    lg = jnp.broadcast_to(lg[:, None, None], (RET_HEADS, 1, RET_DK))
    ret_gain = ret_out_norm.reshape(DEPTH, RET_HEADS, 1, RET_DK)
    w_in_b, w_out_b, w_cq_b, w_co_b = bf(w_in), bf(w_out), bf(w_cq), bf(w_co)
    w_gate_b, w_up_b, w_down_b, w_glu_b = bf(w_gate), bf(w_up), bf(w_down), bf(s5_w_glu)
    g_mix, g_cross, g_ffn = vec(norm_mix), vec(norm_cross), vec(norm_ffn)
    s5_d3, s5_bglu3, s5_gout3 = vec(s5_d), vec(s5_b_glu), vec(s5_out_norm)

    x2d = x.reshape(n, D_MODEL)
    for l in range(DEPTH):
        u, q, k, v, gate = _in_proj(l, x2d, g_mix, w_in_b, cos2, sin2)
        y_ssm = _s5_mixer(l, u.reshape(nb, seq // S5_R, S5_R * S5_WIDTH), s5_bp, a_r, a_i, s5_w,
                          s5_d3, w_glu_b, s5_bglu3, s5_gout3)
        y_ret = _retention(l, q, k, v, gate, lg, ret_gain, nb, seq)
        x2d = _mix_out(l, x2d, y_ssm.reshape(n // S5_R, S5_R * S5_WIDTH), y_ret, w_out_b, g_cross, w_cq_b,
                       k_mem, v_mem, w_co_b, seq // TM)
        x2d = _ffn(l, x2d, g_ffn, w_gate_b, w_up_b, w_down_b, norm_final, l == DEPTH - 1)
    return x2d.reshape(nb, seq, D_MODEL)
```

```python
import functools
import math

import jax
import jax.numpy as jnp
from jax import lax
from jax.experimental import pallas as pl
from jax.experimental.pallas import tpu as pltpu

D_MODEL = 1024
DEPTH = 4
MEM_LEN = 256
S5_WIDTH = 512
S5_GROUP_CH = 16
S5_GROUPS = 32
S5_STATE = 64
RET_HEADS = 4
RET_DK = 128
RET_WIDTH = 512
IN_WIDTH = S5_WIDTH + 4 * RET_WIDTH
X_HEADS = 4
X_HEAD_DIM = 256
D_FF = 2816
ROPE_BASE = 10000.0
EPS = 1e-6

LANES = 128
S5_KT = S5_WIDTH // LANES
S5_GPT = LANES // S5_GROUP_CH
S5_SLAB = S5_GPT * S5_STATE
S5_R = 4
S5_TK = 128
S5_PITCH = S5_TK + 8
RET_CHUNK = 256
RET_TR = 1024
TM = 1024
FF_SPLITS = ((0, 1536), (1536, D_FF))

BF16 = jnp.bfloat16
F32 = jnp.float32
VMEM_LIMIT = 48 * 1024 * 1024


def _dot(a, b):
    return jnp.dot(a, b, preferred_element_type=F32)


def _rms(x, g):
    ms = jnp.mean(x * x, axis=-1, keepdims=True)
    return x * lax.rsqrt(ms + EPS) * g


def _const_spec(shape):
    nd = len(shape)
    return pl.BlockSpec(shape, lambda *_: (0,) * nd, pipeline_mode=pl.Buffered(1))


def _layer_spec(l, shape):
    nd = len(shape)
    return pl.BlockSpec((None,) + tuple(shape), lambda *_: (l,) + (0,) * nd, pipeline_mode=pl.Buffered(1))


def _params(*sem):
    return pltpu.CompilerParams(dimension_semantics=sem, vmem_limit_bytes=VMEM_LIMIT)


def _rope_kernel(pos_ref, invf_ref, sign_ref, c_ref, s_ref):
    ang = pos_ref[...] * invf_ref[...]
    c_ref[...] = jnp.cos(ang)
    s_ref[...] = jnp.sin(ang) * sign_ref[...]


def _rope_tables(positions):
    n = positions.size
    half = RET_DK // 2
    inv_freq = 1.0 / (ROPE_BASE ** (jnp.arange(half, dtype=F32) / half))
    invf2 = jnp.concatenate([inv_freq, inv_freq]).reshape(1, RET_DK)
    sign = jnp.concatenate([-jnp.ones((half,), F32), jnp.ones((half,), F32)]).reshape(1, RET_DK)
    pos = positions.astype(F32).reshape(n, 1)
    t = 1024
    return pl.pallas_call(
        _rope_kernel,
        grid=(n // t,),
        in_specs=[pl.BlockSpec((t, 1), lambda i: (i, 0)),
                  pl.BlockSpec((1, RET_DK), lambda i: (0, 0)),
                  pl.BlockSpec((1, RET_DK), lambda i: (0, 0))],
        out_specs=[pl.BlockSpec((t, RET_DK), lambda i: (i, 0))] * 2,
        out_shape=[jax.ShapeDtypeStruct((n, RET_DK), F32)] * 2,
        compiler_params=_params("arbitrary"),
        name="rope_tables",
    )(pos, invf2, sign)


def _s5_disc_kernel(lr_ref, li_ref, ls_ref, brt_ref, bit_ref, cr_ref, ci_ref,
                    ar_ref, ai_ref, bpr_ref, bpi_ref, cpr_ref, cpi_ref, tk_ref):
    lr = lr_ref[0]
    li = li_ref[0]
    step = jnp.exp(ls_ref[0])
    mag = jnp.exp(lr * step)
    a_r = mag * jnp.cos(li * step)
    a_i = mag * jnp.sin(li * step)
    den = lr * lr + li * li
    f_r = ((a_r - 1.0) * lr + a_i * li) / den
    f_i = (a_i * lr - (a_r - 1.0) * li) / den
    br = brt_ref[0]
    bi = bit_ref[0]
    bb_r = f_r * br - f_i * bi
    bb_i = f_r * bi + f_i * br
    cr = cr_ref[0]
    ci = ci_ref[0]
    contract_p = (((2,), (2,)), ((0,), (0,)))
    pw_r = jnp.ones_like(a_r)
    pw_i = jnp.zeros_like(a_i)
    for m in range(S5_R):
        abb_r = pw_r * bb_r - pw_i * bb_i
        abb_i = pw_r * bb_i + pw_i * bb_r
        bpr_ref[0, S5_R - 1 - m] = abb_r
        bpi_ref[0, S5_R - 1 - m] = abb_i
        tk_ref[0, m] = (
            lax.dot_general(cr, abb_r, contract_p, precision=lax.Precision.HIGHEST, preferred_element_type=F32)
            - lax.dot_general(ci, abb_i, contract_p, precision=lax.Precision.HIGHEST, preferred_element_type=F32))
        pw_r, pw_i = pw_r * a_r - pw_i * a_i, pw_r * a_i + pw_i * a_r
        cpr_ref[0, m] = cr * pw_r - ci * pw_i
        cpi_ref[0, m] = -(cr * pw_i + ci * pw_r)
    ar_ref[0] = pw_r
    ai_ref[0] = pw_i


def _s5_discretise(lam_re, lam_im, log_step, b_re, b_im, c_re, c_im):
    g, p, h = S5_GROUPS, S5_STATE, S5_GROUP_CH
    lr = lam_re.reshape(DEPTH, g, 1, p)
    li = lam_im.reshape(DEPTH, g, 1, p)
    ls = log_step.reshape(DEPTH, g, 1, 1)
    brt = jnp.swapaxes(b_re, -1, -2)
    bit = jnp.swapaxes(b_im, -1, -2)
    vec = pl.BlockSpec((1, g, 1, p), lambda l: (l, 0, 0, 0))
    mat = pl.BlockSpec((1, g, h, p), lambda l: (l, 0, 0, 0))
    rmat = pl.BlockSpec((1, S5_R, g, h, p), lambda l: (l, 0, 0, 0, 0))
    rmat_shape = jax.ShapeDtypeStruct((DEPTH, S5_R, g, h, p), F32)
    return pl.pallas_call(
        _s5_disc_kernel,
        grid=(DEPTH,),
        in_specs=[vec, vec, pl.BlockSpec((1, g, 1, 1), lambda l: (l, 0, 0, 0)), mat, mat, mat, mat],
        out_specs=[vec, vec, rmat, rmat, rmat, rmat,
                   pl.BlockSpec((1, S5_R, g, h, h), lambda l: (l, 0, 0, 0, 0))],
        out_shape=[jax.ShapeDtypeStruct((DEPTH, g, 1, p), F32)] * 2 + [rmat_shape] * 4
        + [jax.ShapeDtypeStruct((DEPTH, S5_R, g, h, h), F32)],
        compiler_params=_params("arbitrary"),
        name="s5_discretise",
    )(lr, li, ls, brt, bit, c_re, c_im)


def _block_diag_tiles(blocks):
    lead = blocks.shape[:-3]
    r, c = blocks.shape[-2:]
    b = blocks.reshape(lead + (S5_KT, S5_GPT, r, c))
    eye = jnp.eye(S5_GPT, dtype=blocks.dtype)
    out = b[..., :, :, None, :] * eye[:, None, :, None]
    return out.reshape(lead + (S5_KT, S5_GPT * r, S5_GPT * c))


def _s5_weights(bp_r, bp_i, cp_r, cp_ni, tk):
    bp_r, bp_i, cp_r, cp_ni, tk = (a.astype(BF16) for a in (bp_r, bp_i, cp_r, cp_ni, tk))
    bp = jnp.concatenate([_block_diag_tiles(bp_r), _block_diag_tiles(bp_i)], axis=-1)
    bp = jnp.moveaxis(bp, 1, 2).reshape(DEPTH, S5_KT, S5_R * LANES, 2 * S5_SLAB)
    def readout(cp):
        t = _block_diag_tiles(jnp.swapaxes(cp, -1, -2))
        return jnp.moveaxis(t, 1, 3).reshape(DEPTH, S5_KT, S5_SLAB, S5_R * LANES)
    t = _block_diag_tiles(jnp.swapaxes(tk, -1, -2))
    zero = jnp.zeros_like(t[:, 0])
    toep = jnp.stack([jnp.stack([t[:, j - i] if j >= i else zero for j in range(S5_R)], axis=3)
                      for i in range(S5_R)], axis=2)
    toep = toep.reshape(DEPTH, S5_KT, S5_R * LANES, S5_R * LANES)
    w = jnp.concatenate([readout(cp_r), readout(cp_ni), toep], axis=2)
    return bp, w


def _memkv_kernel(mem_ref, g_ref, wk_ref, wv_ref, k_ref, v_ref):
    m = _rms(mem_ref[...], g_ref[0]).astype(BF16)
    k_ref[0] = _dot(m, wk_ref[0]).astype(BF16)
    v_ref[0] = _dot(m, wv_ref[0]).astype(BF16)


def _mem_kv(mem2d, norm_mem, w_ck, w_cv):
    rows = mem2d.shape[0]
    wspec = pl.BlockSpec((1, D_MODEL, D_MODEL), lambda l: (l, 0, 0))
    ospec = pl.BlockSpec((1, rows, D_MODEL), lambda l: (l, 0, 0))
    return pl.pallas_call(
        _memkv_kernel,
        grid=(DEPTH,),
        in_specs=[pl.BlockSpec((rows, D_MODEL), lambda l: (0, 0)),
                  pl.BlockSpec((1, 1, D_MODEL), lambda l: (l, 0, 0)), wspec, wspec],
        out_specs=[ospec, ospec],
        out_shape=[jax.ShapeDtypeStruct((DEPTH, rows, D_MODEL), BF16)] * 2,
        compiler_params=_params("arbitrary"),
        name="mem_kv",
    )(mem2d, norm_mem.reshape(DEPTH, 1, D_MODEL), w_ck, w_cv)


def _in_proj_kernel(x_ref, g_ref, w_ref, c_ref, s_ref, u_ref, ub_ref, q_ref, k_ref, v_ref, gate_ref, us_ref):
    h = _rms(x_ref[...], g_ref[...]).astype(BF16)
    cos = c_ref[...]
    sin = s_ref[...]
    u = _dot(h, w_ref[:, 0:S5_WIDTH])
    for c in range(S5_KT):
        us_ref[c] = u[:, c * LANES:(c + 1) * LANES]
    for i in range(S5_R):
        for c in range(S5_KT):
            piece = us_ref[c, pl.ds(i, TM // S5_R, stride=S5_R), :]
            u_ref[:, i * S5_WIDTH + c * LANES:i * S5_WIDTH + (c + 1) * LANES] = piece
            ub_ref[:, (c * S5_R + i) * LANES:(c * S5_R + i + 1) * LANES] = piece.astype(BF16)
    for dst, off, scale in ((q_ref, S5_WIDTH, None), (k_ref, S5_WIDTH + RET_WIDTH, RET_DK ** -0.5)):
        t = _dot(h, w_ref[:, off:off + RET_WIDTH])
        for hd in range(RET_HEADS):
            th = t[:, hd * RET_DK:(hd + 1) * RET_DK]
            r = th * cos + pltpu.roll(th, RET_DK // 2, axis=1) * sin
            if scale is not None:
                r = r * scale
            dst[:, hd * RET_DK:(hd + 1) * RET_DK] = r.astype(BF16)
    off = S5_WIDTH + 2 * RET_WIDTH
    v_ref[...] = _dot(h, w_ref[:, off:off + RET_WIDTH]).astype(BF16)
    gate_ref[...] = _dot(h, w_ref[:, off + RET_WIDTH:off + 2 * RET_WIDTH])


def _blocked_spec(tiles_per_batch):
    return pl.BlockSpec((None, TM // S5_R, S5_R * S5_WIDTH),
                        lambda i: (i // tiles_per_batch, i % tiles_per_batch, 0))


def _in_proj(l, x2d, g, w_bf16, cos2, sin2, nb, seq):
    n = x2d.shape[0]
    row = lambda w: pl.BlockSpec((TM, w), lambda i: (i, 0))
    ret_bf16 = jax.ShapeDtypeStruct((n, RET_WIDTH), BF16)
    return pl.pallas_call(
        _in_proj_kernel,
        grid=(n // TM,),
        in_specs=[row(D_MODEL), _layer_spec(l, (1, D_MODEL)), _layer_spec(l, (D_MODEL, IN_WIDTH)),
                  row(RET_DK), row(RET_DK)],
        out_specs=[_blocked_spec(seq // TM), _blocked_spec(seq // TM)] + [row(RET_WIDTH)] * 4,
        out_shape=[jax.ShapeDtypeStruct((nb, seq // S5_R, S5_R * S5_WIDTH), F32),
                   jax.ShapeDtypeStruct((nb, seq // S5_R, S5_R * S5_WIDTH), BF16),
                   ret_bf16, ret_bf16, ret_bf16,
                   jax.ShapeDtypeStruct((n, RET_WIDTH), F32)],
        scratch_shapes=[pltpu.VMEM((S5_KT, TM, LANES), F32)],
        compiler_params=_params("arbitrary"),
        name="in_proj",
    )(x2d, g, w_bf16, cos2, sin2)


def _s5_kernel(nb, u_ref, ub_ref, bp_ref, ar_ref, ai_ref, w_ref, d_ref, wglu_ref, bglu_ref, gout_ref,
               o_ref, st_ref, y_ref, *xs_refs):
    nslab = S5_SLAB // LANES
    rows = nb * S5_TK

    @pl.when(pl.program_id(0) == 0)
    def _():
        st_ref[...] = jnp.zeros_like(st_ref)

    mxu_n = 2 * LANES

    def block_inputs(kt):
        return ub_ref[:, :, kt * S5_R * LANES:(kt + 1) * S5_R * LANES].reshape(rows, S5_R * LANES)

    def state_input(kt, n):
        xb = _dot(block_inputs(kt), bp_ref[kt, :, n * mxu_n:(n + 1) * mxu_n])
        for b in range(nb):
            for jj in range(mxu_n // LANES):
                xs_refs[kt][n * (mxu_n // LANES) + jj, b * S5_PITCH:b * S5_PITCH + S5_TK, :] = (
                    xb[b * S5_TK:(b + 1) * S5_TK, jj * LANES:(jj + 1) * LANES])

    def readout(kt, n):
        s_prev = jnp.concatenate(
            [jnp.concatenate([xs_refs[kt][j, b * S5_PITCH:b * S5_PITCH + S5_TK, :] for j in range(2 * nslab)],
                             axis=1) for b in range(nb)], axis=0).astype(BF16)
        yk = _dot(jnp.concatenate([s_prev, block_inputs(kt)], axis=1), w_ref[kt, :, n * mxu_n:(n + 1) * mxu_n])
        for jj in range(mxu_n // LANES):
            at = slice((2 * n + jj) * S5_WIDTH + kt * LANES, (2 * n + jj) * S5_WIDTH + (kt + 1) * LANES)
            y = (yk[:, jj * LANES:(jj + 1) * LANES].reshape(nb, S5_TK, LANES)
                 + d_ref[:, kt * LANES:(kt + 1) * LANES] * u_ref[:, :, at])
            y_ref[:, :, at] = 0.5 * y * (1.0 + lax.erf(y * math.sqrt(0.5)))

    def recurrence(kt):
        a_r = [jnp.broadcast_to(ar_ref[kt, :, j * LANES:(j + 1) * LANES], (nb, LANES)) for j in range(nslab)]
        a_i = [jnp.broadcast_to(ai_ref[kt, :, j * LANES:(j + 1) * LANES], (nb, LANES)) for j in range(nslab)]
        s_r = [st_ref[kt, j] for j in range(nslab)]
        s_i = [st_ref[kt, nslab + j] for j in range(nslab)]
        for t in range(S5_TK):
            at = pl.ds(t, nb, stride=S5_PITCH)
            for j in range(nslab):
                x_r = xs_refs[kt][j, at, :]
                x_i = xs_refs[kt][nslab + j, at, :]
                xs_refs[kt][j, at, :] = s_r[j]
                xs_refs[kt][nslab + j, at, :] = s_i[j]
                s_r[j], s_i[j] = (a_r[j] * s_r[j] - a_i[j] * s_i[j] + x_r,
                                  a_r[j] * s_i[j] + a_i[j] * s_r[j] + x_i)
        for j in range(nslab):
            st_ref[kt, j] = s_r[j]
            st_ref[kt, nslab + j] = s_i[j]

    for kt in range(S5_KT):
        for n in range(2 * S5_SLAB // mxu_n):
            state_input(kt, n)
    for kt in range(S5_KT):
        recurrence(kt)
        for n in range(S5_R * LANES // mxu_n):
            readout(kt, n)

    for i in range(S5_R):
        cols = slice(i * S5_WIDTH, (i + 1) * S5_WIDTH)
        z = y_ref[:, :, cols].reshape(rows, S5_WIDTH)
        z = z * jax.nn.sigmoid(_dot(z.astype(BF16), wglu_ref[...]) + bglu_ref[...])
        o_ref[:, :, cols] = _rms(z, gout_ref[...]).astype(BF16).reshape(nb, S5_TK, S5_WIDTH)


def _s5_mixer(l, u3, ub3, bp, a_r, a_i, w, d, wglu, bglu, gout):
    nb, krows, width = u3.shape
    nslab2 = 2 * S5_SLAB // LANES
    tok = pl.BlockSpec((nb, S5_TK, width), lambda c: (0, c, 0))
    return pl.pallas_call(
        functools.partial(_s5_kernel, nb),
        grid=(krows // S5_TK,),
        in_specs=[tok, tok,
                  _layer_spec(l, (S5_KT, S5_R * LANES, 2 * S5_SLAB)),
                  _layer_spec(l, (S5_KT, 1, S5_SLAB)), _layer_spec(l, (S5_KT, 1, S5_SLAB)),
                  _layer_spec(l, (S5_KT, 2 * S5_SLAB + S5_R * LANES, S5_R * LANES)),
                  _layer_spec(l, (1, S5_WIDTH)), _layer_spec(l, (S5_WIDTH, S5_WIDTH)),
                  _layer_spec(l, (1, S5_WIDTH)), _layer_spec(l, (1, S5_WIDTH))],
        out_specs=tok,
        out_shape=jax.ShapeDtypeStruct((nb, krows, width), BF16),
        scratch_shapes=[pltpu.VMEM((S5_KT, nslab2, nb, LANES), F32),
                        pltpu.VMEM((nb, S5_TK, width), F32)]
        + [pltpu.VMEM((nslab2, nb * S5_PITCH, LANES), F32)] * S5_KT,
        compiler_params=_params("arbitrary"),
        name="s5_mixer",
    )(u3, ub3, bp, a_r, a_i, w, d, wglu, bglu, gout)


def _ret_kernel(q_ref, k_ref, v_ref, g_ref, lg_ref, gain_ref, o_ref, st_ref, dm_ref, zeta_ref, xi_ref):
    @pl.when((pl.program_id(0) == 0) & (pl.program_id(1) == 0))
    def _():
        row = lax.broadcasted_iota(jnp.int32, (RET_CHUNK, RET_CHUNK), 0).astype(F32)
        col = lax.broadcasted_iota(jnp.int32, (RET_CHUNK, RET_CHUNK), 1).astype(F32)
        diff = row - col
        idx = lax.broadcasted_iota(jnp.int32, (RET_CHUNK, RET_DK), 0).astype(F32)
        for hd in range(RET_HEADS):
            lg = lg_ref[hd]
            lg_wide = jnp.concatenate([lg] * (RET_CHUNK // RET_DK), axis=1)
            dm_ref[hd] = jnp.where(diff >= 0, jnp.exp(jnp.maximum(diff, 0.0) * lg_wide), 0.0)
            zeta_ref[hd] = jnp.exp((RET_CHUNK - 1.0 - idx) * lg)
            xi_ref[hd] = jnp.exp((idx + 1.0) * lg)

    @pl.when(pl.program_id(1) == 0)
    def _():
        st_ref[...] = jnp.zeros_like(st_ref)

    for c in range(RET_TR // RET_CHUNK):
        rows = slice(c * RET_CHUNK, (c + 1) * RET_CHUNK)
        for hd in range(RET_HEADS):
            lanes = slice(hd * RET_DK, (hd + 1) * RET_DK)
            qh = q_ref[rows, lanes]
            kh = k_ref[rows, lanes]
            vh = v_ref[rows, lanes]
            s = lax.dot_general(qh, kh, (((1,), (1,)), ((), ())), preferred_element_type=F32)
            inner = _dot((s * dm_ref[hd]).astype(BF16), vh)
            state = st_ref[hd]
            cross = _dot(qh, state.astype(BF16)) * xi_ref[hd]
            kz = (kh.astype(F32) * zeta_ref[hd]).astype(BF16)
            kv = lax.dot_general(kz, vh, (((0,), (0,)), ((), ())), preferred_element_type=F32)
            st_ref[hd] = jnp.exp(RET_CHUNK * lg_ref[hd]) * state + kv
            o = inner + cross
            mu = jnp.mean(o, axis=-1, keepdims=True)
            oc = o - mu
            var = jnp.mean(oc * oc, axis=-1, keepdims=True)
            o = oc * lax.rsqrt(var + EPS) * gain_ref[hd]
            o_ref[rows, lanes] = (o * jax.nn.silu(g_ref[rows, lanes])).astype(BF16)


def _retention(l, q, k, v, gate, lg, gain, nb, seq):
    steps = seq // RET_TR
    tok = pl.BlockSpec((RET_TR, RET_WIDTH), lambda b, c: (b * steps + c, 0))
    return pl.pallas_call(
        _ret_kernel,
        grid=(nb, steps),
        in_specs=[tok, tok, tok, tok, _const_spec((RET_HEADS, 1, RET_DK)),
                  _layer_spec(l, (RET_HEADS, 1, RET_DK))],
        out_specs=tok,
        out_shape=jax.ShapeDtypeStruct((nb * seq, RET_WIDTH), BF16),
        scratch_shapes=[pltpu.VMEM((RET_HEADS, RET_DK, RET_DK), F32),
                        pltpu.VMEM((RET_HEADS, RET_CHUNK, RET_CHUNK), F32),
                        pltpu.VMEM((RET_HEADS, RET_CHUNK, RET_DK), F32),
                        pltpu.VMEM((RET_HEADS, RET_CHUNK, RET_DK), F32)],
        compiler_params=_params("arbitrary", "arbitrary"),
        name="retention",
    )(q, k, v, gate, lg, gain)


def _mix_out_kernel(x_ref, ys_ref, yr_ref, wo_ref, g_ref, wq_ref, k_ref, v_ref, wco_ref, o_ref, ps_ref):
    tk = TM // S5_R
    ys = jnp.concatenate([ys_ref[:, i * S5_WIDTH:(i + 1) * S5_WIDTH] for i in range(S5_R)], axis=0)
    ps = _dot(ys, wo_ref[0:S5_WIDTH, :])
    for i in range(S5_R):
        for c in range(D_MODEL // LANES):
            ps_ref[c, pl.ds(i, tk, stride=S5_R), :] = ps[i * tk:(i + 1) * tk, c * LANES:(c + 1) * LANES]
    ssm = jnp.concatenate([ps_ref[c] for c in range(D_MODEL // LANES)], axis=1)
    x = x_ref[...] + ssm + _dot(yr_ref[...], wo_ref[S5_WIDTH:, :])
    h = _rms(x, g_ref[...]).astype(BF16)
    q = _dot(h, wq_ref[...]).astype(BF16)
    outs = []
    for hd in range(X_HEADS):
        lanes = slice(hd * X_HEAD_DIM, (hd + 1) * X_HEAD_DIM)
        s = lax.dot_general(q[:, lanes], k_ref[:, lanes], (((1,), (1,)), ((), ())),
                            preferred_element_type=F32) * (X_HEAD_DIM ** -0.5)
        e = jnp.exp(s - jnp.max(s, axis=-1, keepdims=True))
        p = e * (1.0 / jnp.sum(e, axis=-1, keepdims=True))
        outs.append(_dot(p.astype(BF16), v_ref[:, lanes]).astype(BF16))
    o = jnp.concatenate(outs, axis=1)
    o_ref[...] = x + _dot(o, wco_ref[...])


def _mix_out(l, x2d, y_ssm, y_ret, w_out, g, wq, k_mem, v_mem, wco, tiles_per_batch):
    n = x2d.shape[0]
    row = lambda w: pl.BlockSpec((TM, w), lambda i: (i, 0))
    mem = pl.BlockSpec((None, MEM_LEN, D_MODEL), lambda i: (l, i // tiles_per_batch, 0))
    sq = _layer_spec(l, (D_MODEL, D_MODEL))
    return pl.pallas_call(
        _mix_out_kernel,
        grid=(n // TM,),
        in_specs=[row(D_MODEL), _blocked_spec(tiles_per_batch),
                  row(RET_WIDTH), sq, _layer_spec(l, (1, D_MODEL)), sq, mem, mem, sq],
        out_specs=row(D_MODEL),
        out_shape=jax.ShapeDtypeStruct((n, D_MODEL), F32),
        scratch_shapes=[pltpu.VMEM((D_MODEL // LANES, TM, LANES), F32)],
        compiler_params=_params("arbitrary"),
        name="mix_out_xattn",
    )(x2d, y_ssm, y_ret, w_out, g, wq, k_mem, v_mem, wco)


def _ffn_kernel(final, x_ref, g_ref, wg_ref, wu_ref, wd_ref, gf_ref, o_ref):
    x = x_ref[...]
    h = _rms(x, g_ref[...]).astype(BF16)
    acc = x
    for lo, hi in FF_SPLITS:
        act = jax.nn.silu(_dot(h, wg_ref[:, lo:hi])) * _dot(h, wu_ref[:, lo:hi])
        acc = acc + _dot(act.astype(BF16), wd_ref[lo:hi, :])
    if final:
        acc = _rms(acc, gf_ref[...])
    o_ref[...] = acc


def _ffn(l, x2d, g, wg, wu, wd, g_final, final):
    n = x2d.shape[0]
    row = pl.BlockSpec((TM, D_MODEL), lambda i: (i, 0))
    return pl.pallas_call(
        functools.partial(_ffn_kernel, final),
        grid=(n // TM,),
        in_specs=[row, _layer_spec(l, (1, D_MODEL)), _layer_spec(l, (D_MODEL, D_FF)),
                  _layer_spec(l, (D_MODEL, D_FF)), _layer_spec(l, (D_FF, D_MODEL)), _const_spec((1, D_MODEL))],
        out_specs=row,
        out_shape=jax.ShapeDtypeStruct((n, D_MODEL), F32),
        compiler_params=_params("arbitrary"),
        name="ffn_final" if final else "ffn",
    )(x2d, g, wg, wu, wd, g_final.reshape(1, D_MODEL))


def kernel(x, mem, positions, norm_mix, w_in, s5_lambda_re, s5_lambda_im, s5_log_step, s5_b_re, s5_b_im, s5_c_re, s5_c_im, s5_d, s5_w_glu, s5_b_glu, s5_out_norm, ret_out_norm, w_out, norm_cross, norm_mem, w_cq, w_ck, w_cv, w_co, norm_ffn, w_gate, w_up, w_down, norm_final):
    nb, seq, _ = x.shape
    n = nb * seq
    assert seq % RET_TR == 0 and seq % (S5_R * S5_TK) == 0 and seq % TM == 0
    bf = lambda a: a.astype(BF16)
    vec = lambda a: a.reshape(DEPTH, 1, a.shape[-1])

    cos2, sin2 = _rope_tables(positions)
    a_r, a_i, bp_r, bp_i, cp_r, cp_ni, tk = _s5_discretise(
        s5_lambda_re, s5_lambda_im, s5_log_step, s5_b_re, s5_b_im, s5_c_re, s5_c_im)
    a_r = a_r.reshape(DEPTH, S5_KT, 1, S5_SLAB)
    a_i = a_i.reshape(DEPTH, S5_KT, 1, S5_SLAB)
    s5_bp, s5_w = _s5_weights(bp_r, bp_i, cp_r, cp_ni, tk)
    k_mem, v_mem = _mem_kv(mem.reshape(nb * MEM_LEN, D_MODEL), norm_mem, bf(w_ck), bf(w_cv))
    lg = jnp.log1p(-jnp.exp2(-5.0 - jnp.arange(RET_HEADS, dtype=F32)))
    lg = jnp.broadcast_to(lg[:, None, None], (RET_HEADS, 1, RET_DK))
    ret_gain = ret_out_norm.reshape(DEPTH, RET_HEADS, 1, RET_DK)
    w_in_b, w_out_b, w_cq_b, w_co_b = bf(w_in), bf(w_out), bf(w_cq), bf(w_co)
    w_gate_b, w_up_b, w_down_b, w_glu_b = bf(w_gate), bf(w_up), bf(w_down), bf(s5_w_glu)
    g_mix, g_cross, g_ffn = vec(norm_mix), vec(norm_cross), vec(norm_ffn)
    s5_d3, s5_bglu3, s5_gout3 = vec(s5_d), vec(s5_b_glu), vec(s5_out_norm)

    x2d = x.reshape(n, D_MODEL)
    for l in range(DEPTH):
        u, ub, q, k, v, gate = _in_proj(l, x2d, g_mix, w_in_b, cos2, sin2, nb, seq)
        y_ssm = _s5_mixer(l, u, ub, s5_bp, a_r, a_i, s5_w,
                          s5_d3, w_glu_b, s5_bglu3, s5_gout3)
        y_ret = _retention(l, q, k, v, gate, lg, ret_gain, nb, seq)
        x2d = _mix_out(l, x2d, y_ssm, y_ret, w_out_b, g_cross, w_cq_b,
                       k_mem, v_mem, w_co_b, seq // TM)
        x2d = _ffn(l, x2d, g_ffn, w_gate_b, w_up_b, w_down_b, norm_final, l == DEPTH - 1)
    return x2d.reshape(nb, seq, D_MODEL)
```

```python
import functools
import math

import jax
import jax.numpy as jnp
from jax import lax
from jax.experimental import pallas as pl
from jax.experimental.pallas import tpu as pltpu

D_MODEL = 1024
DEPTH = 4
MEM_LEN = 256
S5_WIDTH = 512
S5_GROUP_CH = 16
S5_GROUPS = 32
S5_STATE = 64
RET_HEADS = 4
RET_DK = 128
RET_WIDTH = 512
IN_WIDTH = S5_WIDTH + 4 * RET_WIDTH
X_HEADS = 4
X_HEAD_DIM = 256
D_FF = 2816
ROPE_BASE = 10000.0
EPS = 1e-6

LANES = 128
SUBLANES = 8
S5_KT = S5_WIDTH // LANES
S5_GPT = LANES // S5_GROUP_CH
S5_SLAB = S5_GPT * S5_STATE
S5_R = 4
S5_TK = 128
S5_PITCH = S5_TK + 8
RET_CHUNK = 256
RET_TR = 1024
TM = 1024
FF_SPLITS = ((0, 1536), (1536, D_FF))

BF16 = jnp.bfloat16
F32 = jnp.float32
VMEM_LIMIT = 48 * 1024 * 1024


def _dot(a, b):
    return jnp.dot(a, b, preferred_element_type=F32)


def _rms(x, g):
    ms = jnp.mean(x * x, axis=-1, keepdims=True)
    return x * lax.rsqrt(ms + EPS) * g


def _const_spec(shape):
    nd = len(shape)
    return pl.BlockSpec(shape, lambda *_: (0,) * nd, pipeline_mode=pl.Buffered(1))


def _layer_spec(l, shape):
    nd = len(shape)
    return pl.BlockSpec((None,) + tuple(shape), lambda *_: (l,) + (0,) * nd, pipeline_mode=pl.Buffered(1))


def _params(*sem):
    return pltpu.CompilerParams(dimension_semantics=sem, vmem_limit_bytes=VMEM_LIMIT)


def _rope_kernel(pos_ref, invf_ref, sign_ref, c_ref, s_ref):
    ang = pos_ref[...] * invf_ref[...]
    c_ref[...] = jnp.cos(ang)
    s_ref[...] = jnp.sin(ang) * sign_ref[...]


def _rope_tables(positions):
    n = positions.size
    half = RET_DK // 2
    inv_freq = 1.0 / (ROPE_BASE ** (jnp.arange(half, dtype=F32) / half))
    invf2 = jnp.concatenate([inv_freq, inv_freq]).reshape(1, RET_DK)
    sign = jnp.concatenate([-jnp.ones((half,), F32), jnp.ones((half,), F32)]).reshape(1, RET_DK)
    pos = positions.astype(F32).reshape(n, 1)
    t = 1024
    return pl.pallas_call(
        _rope_kernel,
        grid=(n // t,),
        in_specs=[pl.BlockSpec((t, 1), lambda i: (i, 0)),
                  pl.BlockSpec((1, RET_DK), lambda i: (0, 0)),
                  pl.BlockSpec((1, RET_DK), lambda i: (0, 0))],
        out_specs=[pl.BlockSpec((t, RET_DK), lambda i: (i, 0))] * 2,
        out_shape=[jax.ShapeDtypeStruct((n, RET_DK), F32)] * 2,
        compiler_params=_params("arbitrary"),
        name="rope_tables",
    )(pos, invf2, sign)


def _tile_lanes(x, reps):
    while reps > 1:
        x = jnp.concatenate([x, x], axis=1)
        reps //= 2
    return x


def _block_diag(blocks, kt):
    _, r, c = blocks.shape
    x = _tile_lanes(blocks[kt * S5_GPT:(kt + 1) * S5_GPT].reshape(S5_GPT * r, c), S5_GPT)
    row_group = lax.broadcasted_iota(jnp.int32, x.shape, 0) // r
    col_group = lax.broadcasted_iota(jnp.int32, x.shape, 1) // c
    return jnp.where(row_group == col_group, x, 0.0)


def _s5_disc_kernel(lr_ref, li_ref, ls_ref, brt_ref, bit_ref, cr_ref, ci_ref, ar_ref, ai_ref, bp_ref, w_ref):
    lr = lr_ref[0]
    li = li_ref[0]
    step = jnp.exp(ls_ref[0])
    mag = jnp.exp(lr * step)
    a_r = mag * jnp.cos(li * step)
    a_i = mag * jnp.sin(li * step)
    den = lr * lr + li * li
    f_r = ((a_r - 1.0) * lr + a_i * li) / den
    f_i = (a_i * lr - (a_r - 1.0) * li) / den
    br = brt_ref[0]
    bi = bit_ref[0]
    bb_r = f_r * br - f_i * bi
    bb_i = f_r * bi + f_i * br
    cr = cr_ref[0]
    ci = ci_ref[0]
    contract_p = (((2,), (2,)), ((0,), (0,)))
    hi = dict(precision=lax.Precision.HIGHEST, preferred_element_type=F32)
    rows_s, rows_u = 2 * S5_SLAB, S5_R * LANES
    pw_r = jnp.ones_like(a_r)
    pw_i = jnp.zeros_like(a_i)
    for m in range(S5_R):
        abb_r = pw_r * bb_r - pw_i * bb_i
        abb_i = pw_r * bb_i + pw_i * bb_r
        lag = lax.dot_general(abb_r, cr, contract_p, **hi) - lax.dot_general(abb_i, ci, contract_p, **hi)
        pw_r, pw_i = pw_r * a_r - pw_i * a_i, pw_r * a_i + pw_i * a_r
        cp_r = cr * pw_r - ci * pw_i
        cp_i = cr * pw_i + ci * pw_r
        for kt in range(S5_KT):
            i = S5_R - 1 - m
            bp_ref[0, kt, i * LANES:(i + 1) * LANES, 0:S5_SLAB] = _block_diag(abb_r, kt).astype(BF16)
            bp_ref[0, kt, i * LANES:(i + 1) * LANES, S5_SLAB:] = _block_diag(abb_i, kt).astype(BF16)
            cols = slice(m * LANES, (m + 1) * LANES)
            w_ref[0, kt, 0:S5_SLAB, cols] = _block_diag(cp_r, kt).T.astype(BF16)
            w_ref[0, kt, S5_SLAB:rows_s, cols] = (-_block_diag(cp_i, kt)).T.astype(BF16)
            lag_kt = _block_diag(lag, kt).astype(BF16)
            for src in range(S5_R - m):
                w_ref[0, kt, rows_s + src * LANES:rows_s + (src + 1) * LANES,
                      (src + m) * LANES:(src + m + 1) * LANES] = lag_kt
            if m + 1 < S5_R:
                w_ref[0, kt, rows_s + (m + 1) * LANES:rows_s + rows_u, cols] = (
                    jnp.zeros(((S5_R - 1 - m) * LANES, LANES), BF16))
    ar_ref[0] = pw_r
    ai_ref[0] = pw_i


def _s5_discretise(lam_re, lam_im, log_step, b_re, b_im, c_re, c_im):
    g, p, h = S5_GROUPS, S5_STATE, S5_GROUP_CH
    lr = lam_re.reshape(DEPTH, g, 1, p)
    li = lam_im.reshape(DEPTH, g, 1, p)
    ls = log_step.reshape(DEPTH, g, 1, 1)
    brt = jnp.swapaxes(b_re, -1, -2)
    bit = jnp.swapaxes(b_im, -1, -2)
    vec = pl.BlockSpec((1, g, 1, p), lambda l: (l, 0, 0, 0))
    mat = pl.BlockSpec((1, g, h, p), lambda l: (l, 0, 0, 0))
    bp_shape = (S5_KT, S5_R * LANES, 2 * S5_SLAB)
    w_shape = (S5_KT, 2 * S5_SLAB + S5_R * LANES, S5_R * LANES)
    return pl.pallas_call(
        _s5_disc_kernel,
        grid=(DEPTH,),
        in_specs=[vec, vec, pl.BlockSpec((1, g, 1, 1), lambda l: (l, 0, 0, 0)), mat, mat, mat, mat],
        out_specs=[vec, vec, pl.BlockSpec((1,) + bp_shape, lambda l: (l, 0, 0, 0)),
                   pl.BlockSpec((1,) + w_shape, lambda l: (l, 0, 0, 0))],
        out_shape=[jax.ShapeDtypeStruct((DEPTH, g, 1, p), F32)] * 2
        + [jax.ShapeDtypeStruct((DEPTH,) + bp_shape, BF16), jax.ShapeDtypeStruct((DEPTH,) + w_shape, BF16)],
        compiler_params=_params("arbitrary"),
        name="s5_discretise",
    )(lr, li, ls, brt, bit, c_re, c_im)


def _memkv_kernel(mem_ref, g_ref, wk_ref, wv_ref, k_ref, v_ref):
    m = _rms(mem_ref[...], g_ref[0]).astype(BF16)
    k_ref[0] = _dot(m, wk_ref[0]).astype(BF16)
    v_ref[0] = _dot(m, wv_ref[0]).astype(BF16)


def _mem_kv(mem2d, norm_mem, w_ck, w_cv):
    rows = mem2d.shape[0]
    wspec = pl.BlockSpec((1, D_MODEL, D_MODEL), lambda l: (l, 0, 0))
    ospec = pl.BlockSpec((1, rows, D_MODEL), lambda l: (l, 0, 0))
    return pl.pallas_call(
        _memkv_kernel,
        grid=(DEPTH,),
        in_specs=[pl.BlockSpec((rows, D_MODEL), lambda l: (0, 0)),
                  pl.BlockSpec((1, 1, D_MODEL), lambda l: (l, 0, 0)), wspec, wspec],
        out_specs=[ospec, ospec],
        out_shape=[jax.ShapeDtypeStruct((DEPTH, rows, D_MODEL), BF16)] * 2,
        compiler_params=_params("arbitrary"),
        name="mem_kv",
    )(mem2d, norm_mem.reshape(DEPTH, 1, D_MODEL), w_ck, w_cv)


def _in_proj_kernel(x_ref, g_ref, w_ref, c_ref, s_ref, u_ref, ub_ref, q_ref, k_ref, v_ref, gate_ref, us_ref):
    h = _rms(x_ref[...], g_ref[...]).astype(BF16)
    cos = c_ref[...]
    sin = s_ref[...]
    u = _dot(h, w_ref[:, 0:S5_WIDTH])
    for c in range(S5_KT):
        us_ref[c] = u[:, c * LANES:(c + 1) * LANES]
    for i in range(S5_R):
        for c in range(S5_KT):
            piece = us_ref[c, pl.ds(i, TM // S5_R, stride=S5_R), :]
            u_ref[:, i * S5_WIDTH + c * LANES:i * S5_WIDTH + (c + 1) * LANES] = piece
            ub_ref[:, (c * S5_R + i) * LANES:(c * S5_R + i + 1) * LANES] = piece.astype(BF16)
    for dst, off, scale in ((q_ref, S5_WIDTH, None), (k_ref, S5_WIDTH + RET_WIDTH, RET_DK ** -0.5)):
        t = _dot(h, w_ref[:, off:off + RET_WIDTH])
        for hd in range(RET_HEADS):
            th = t[:, hd * RET_DK:(hd + 1) * RET_DK]
            r = th * cos + pltpu.roll(th, RET_DK // 2, axis=1) * sin
            if scale is not None:
                r = r * scale
            dst[:, hd * RET_DK:(hd + 1) * RET_DK] = r.astype(BF16)
    off = S5_WIDTH + 2 * RET_WIDTH
    v_ref[...] = _dot(h, w_ref[:, off:off + RET_WIDTH]).astype(BF16)
    gate_ref[...] = _dot(h, w_ref[:, off + RET_WIDTH:off + 2 * RET_WIDTH])


def _blocked_spec(tiles_per_batch):
    return pl.BlockSpec((None, TM // S5_R, S5_R * S5_WIDTH),
                        lambda i: (i // tiles_per_batch, i % tiles_per_batch, 0))


def _in_proj(l, x2d, g, w_bf16, cos2, sin2, nb, seq):
    n = x2d.shape[0]
    row = lambda w: pl.BlockSpec((TM, w), lambda i: (i, 0))
    ret_bf16 = jax.ShapeDtypeStruct((n, RET_WIDTH), BF16)
    return pl.pallas_call(
        _in_proj_kernel,
        grid=(n // TM,),
        in_specs=[row(D_MODEL), _layer_spec(l, (1, D_MODEL)), _layer_spec(l, (D_MODEL, IN_WIDTH)),
                  row(RET_DK), row(RET_DK)],
        out_specs=[_blocked_spec(seq // TM), _blocked_spec(seq // TM)] + [row(RET_WIDTH)] * 4,
        out_shape=[jax.ShapeDtypeStruct((nb, seq // S5_R, S5_R * S5_WIDTH), F32),
                   jax.ShapeDtypeStruct((nb, seq // S5_R, S5_R * S5_WIDTH), BF16),
                   ret_bf16, ret_bf16, ret_bf16,
                   jax.ShapeDtypeStruct((n, RET_WIDTH), F32)],
        scratch_shapes=[pltpu.VMEM((S5_KT, TM, LANES), F32)],
        compiler_params=_params("arbitrary"),
        name="in_proj",
    )(x2d, g, w_bf16, cos2, sin2)


def _s5_kernel(nb, u_ref, ub_ref, bp_ref, ar_ref, ai_ref, w_ref, d_ref, wglu_ref, bglu_ref, gout_ref,
               o_ref, st_ref, y_ref, *xs_refs):
    nslab = S5_SLAB // LANES
    rows = nb * S5_TK

    @pl.when(pl.program_id(0) == 0)
    def _():
        st_ref[...] = jnp.zeros_like(st_ref)

    mxu_n = 2 * LANES

    pack = SUBLANES // nb
    npk = nslab // pack

    def slab_block(s, b):
        return s // pack, ((s % pack) * nb + b) * S5_PITCH

    def block_inputs(kt):
        return ub_ref[:, :, kt * S5_R * LANES:(kt + 1) * S5_R * LANES].reshape(rows, S5_R * LANES)

    def state_input(kt, n):
        xb = _dot(block_inputs(kt), bp_ref[kt, :, n * mxu_n:(n + 1) * mxu_n])
        for b in range(nb):
            for jj in range(mxu_n // LANES):
                p, r0 = slab_block(n * (mxu_n // LANES) + jj, b)
                xs_refs[kt][p, r0:r0 + S5_TK, :] = xb[b * S5_TK:(b + 1) * S5_TK, jj * LANES:(jj + 1) * LANES]

    def readout(kt, n):
        def slab(s, b):
            p, r0 = slab_block(s, b)
            return xs_refs[kt][p, r0:r0 + S5_TK, :]

        s_prev = jnp.concatenate(
            [jnp.concatenate([slab(s, b) for s in range(2 * nslab)], axis=1) for b in range(nb)],
            axis=0).astype(BF16)
        yk = _dot(jnp.concatenate([s_prev, block_inputs(kt)], axis=1), w_ref[kt, :, n * mxu_n:(n + 1) * mxu_n])
        for jj in range(mxu_n // LANES):
            at = slice((2 * n + jj) * S5_WIDTH + kt * LANES, (2 * n + jj) * S5_WIDTH + (kt + 1) * LANES)
            y = (yk[:, jj * LANES:(jj + 1) * LANES].reshape(nb, S5_TK, LANES)
                 + d_ref[:, kt * LANES:(kt + 1) * LANES] * u_ref[:, :, at])
            y_ref[:, :, at] = 0.5 * y * (1.0 + lax.erf(y * math.sqrt(0.5)))

    def recurrence(kt):
        def packed(ref, q):
            return jnp.concatenate(
                [jnp.broadcast_to(ref[kt, :, s * LANES:(s + 1) * LANES], (nb, LANES))
                 for s in range(q * pack, (q + 1) * pack)], axis=0)

        a_r = [packed(ar_ref, q) for q in range(npk)]
        a_i = [packed(ai_ref, q) for q in range(npk)]
        s_r = [st_ref[kt, q] for q in range(npk)]
        s_i = [st_ref[kt, npk + q] for q in range(npk)]
        for t in range(S5_TK):
            at = pl.ds(t, pack * nb, stride=S5_PITCH)
            for q in range(npk):
                x_r = xs_refs[kt][q, at, :]
                x_i = xs_refs[kt][npk + q, at, :]
                xs_refs[kt][q, at, :] = s_r[q]
                xs_refs[kt][npk + q, at, :] = s_i[q]
                s_r[q], s_i[q] = (a_r[q] * s_r[q] - a_i[q] * s_i[q] + x_r,
                                  a_r[q] * s_i[q] + a_i[q] * s_r[q] + x_i)
        for q in range(npk):
            st_ref[kt, q] = s_r[q]
            st_ref[kt, npk + q] = s_i[q]

    for kt in range(S5_KT):
        for n in range(2 * S5_SLAB // mxu_n):
            state_input(kt, n)
    for kt in range(S5_KT):
        recurrence(kt)
        for n in range(S5_R * LANES // mxu_n):
            readout(kt, n)

    for i in range(S5_R):
        cols = slice(i * S5_WIDTH, (i + 1) * S5_WIDTH)
        z = y_ref[:, :, cols].reshape(rows, S5_WIDTH)
        z = z * jax.nn.sigmoid(_dot(z.astype(BF16), wglu_ref[...]) + bglu_ref[...])
        o_ref[:, :, cols] = _rms(z, gout_ref[...]).astype(BF16).reshape(nb, S5_TK, S5_WIDTH)


def _s5_mixer(l, u3, ub3, bp, a_r, a_i, w, d, wglu, bglu, gout):
    nb, krows, width = u3.shape
    assert SUBLANES % nb == 0
    nbuf = 2 * S5_SLAB // LANES * nb // SUBLANES
    tok = pl.BlockSpec((nb, S5_TK, width), lambda c: (0, c, 0))
    return pl.pallas_call(
        functools.partial(_s5_kernel, nb),
        grid=(krows // S5_TK,),
        in_specs=[tok, tok,
                  _layer_spec(l, (S5_KT, S5_R * LANES, 2 * S5_SLAB)),
                  _layer_spec(l, (S5_KT, 1, S5_SLAB)), _layer_spec(l, (S5_KT, 1, S5_SLAB)),
                  _layer_spec(l, (S5_KT, 2 * S5_SLAB + S5_R * LANES, S5_R * LANES)),
                  _layer_spec(l, (1, S5_WIDTH)), _layer_spec(l, (S5_WIDTH, S5_WIDTH)),
                  _layer_spec(l, (1, S5_WIDTH)), _layer_spec(l, (1, S5_WIDTH))],
        out_specs=tok,
        out_shape=jax.ShapeDtypeStruct((nb, krows, width), BF16),
        scratch_shapes=[pltpu.VMEM((S5_KT, nbuf, SUBLANES, LANES), F32),
                        pltpu.VMEM((nb, S5_TK, width), F32)]
        + [pltpu.VMEM((nbuf, SUBLANES * S5_PITCH, LANES), F32)] * S5_KT,
        compiler_params=_params("arbitrary"),
        name="s5_mixer",
    )(u3, ub3, bp, a_r, a_i, w, d, wglu, bglu, gout)


def _ret_kernel(q_ref, k_ref, v_ref, g_ref, lg_ref, gain_ref, o_ref, st_ref, dm_ref, zeta_ref, xi_ref):
    @pl.when((pl.program_id(0) == 0) & (pl.program_id(1) == 0))
    def _():
        row = lax.broadcasted_iota(jnp.int32, (RET_CHUNK, RET_CHUNK), 0).astype(F32)
        col = lax.broadcasted_iota(jnp.int32, (RET_CHUNK, RET_CHUNK), 1).astype(F32)
        diff = row - col
        idx = lax.broadcasted_iota(jnp.int32, (RET_CHUNK, RET_DK), 0).astype(F32)
        for hd in range(RET_HEADS):
            lg = lg_ref[hd]
            lg_wide = jnp.concatenate([lg] * (RET_CHUNK // RET_DK), axis=1)
            dm_ref[hd] = jnp.where(diff >= 0, jnp.exp(jnp.maximum(diff, 0.0) * lg_wide), 0.0)
            zeta_ref[hd] = jnp.exp((RET_CHUNK - 1.0 - idx) * lg)
            xi_ref[hd] = jnp.exp((idx + 1.0) * lg)

    @pl.when(pl.program_id(1) == 0)
    def _():
        st_ref[...] = jnp.zeros_like(st_ref)

    for c in range(RET_TR // RET_CHUNK):
        rows = slice(c * RET_CHUNK, (c + 1) * RET_CHUNK)
        for hd in range(RET_HEADS):
            lanes = slice(hd * RET_DK, (hd + 1) * RET_DK)
            qh = q_ref[rows, lanes]
            kh = k_ref[rows, lanes]
            vh = v_ref[rows, lanes]
            s = lax.dot_general(qh, kh, (((1,), (1,)), ((), ())), preferred_element_type=F32)
            inner = _dot((s * dm_ref[hd]).astype(BF16), vh)
            state = st_ref[hd]
            cross = _dot(qh, state.astype(BF16)) * xi_ref[hd]
            kz = (kh.astype(F32) * zeta_ref[hd]).astype(BF16)
            kv = lax.dot_general(kz, vh, (((0,), (0,)), ((), ())), preferred_element_type=F32)
            st_ref[hd] = jnp.exp(RET_CHUNK * lg_ref[hd]) * state + kv
            o = inner + cross
            mu = jnp.mean(o, axis=-1, keepdims=True)
            oc = o - mu
            var = jnp.mean(oc * oc, axis=-1, keepdims=True)
            o = oc * lax.rsqrt(var + EPS) * gain_ref[hd]
            o_ref[rows, lanes] = (o * jax.nn.silu(g_ref[rows, lanes])).astype(BF16)


def _retention(l, q, k, v, gate, lg, gain, nb, seq):
    steps = seq // RET_TR
    tok = pl.BlockSpec((RET_TR, RET_WIDTH), lambda b, c: (b * steps + c, 0))
    return pl.pallas_call(
        _ret_kernel,
        grid=(nb, steps),
        in_specs=[tok, tok, tok, tok, _const_spec((RET_HEADS, 1, RET_DK)),
                  _layer_spec(l, (RET_HEADS, 1, RET_DK))],
        out_specs=tok,
        out_shape=jax.ShapeDtypeStruct((nb * seq, RET_WIDTH), BF16),
        scratch_shapes=[pltpu.VMEM((RET_HEADS, RET_DK, RET_DK), F32),
                        pltpu.VMEM((RET_HEADS, RET_CHUNK, RET_CHUNK), F32),
                        pltpu.VMEM((RET_HEADS, RET_CHUNK, RET_DK), F32),
                        pltpu.VMEM((RET_HEADS, RET_CHUNK, RET_DK), F32)],
        compiler_params=_params("arbitrary", "arbitrary"),
        name="retention",
    )(q, k, v, gate, lg, gain)


def _mix_out_kernel(x_ref, ys_ref, yr_ref, wo_ref, g_ref, wq_ref, k_ref, v_ref, wco_ref, o_ref, ps_ref):
    tk = TM // S5_R
    ys = jnp.concatenate([ys_ref[:, i * S5_WIDTH:(i + 1) * S5_WIDTH] for i in range(S5_R)], axis=0)
    ps = _dot(ys, wo_ref[0:S5_WIDTH, :])
    for i in range(S5_R):
        for c in range(D_MODEL // LANES):
            ps_ref[c, pl.ds(i, tk, stride=S5_R), :] = ps[i * tk:(i + 1) * tk, c * LANES:(c + 1) * LANES]
    ssm = jnp.concatenate([ps_ref[c] for c in range(D_MODEL // LANES)], axis=1)
    x = x_ref[...] + ssm + _dot(yr_ref[...], wo_ref[S5_WIDTH:, :])
    h = _rms(x, g_ref[...]).astype(BF16)
    q = _dot(h, wq_ref[...]).astype(BF16)
    outs = []
    for hd in range(X_HEADS):
        lanes = slice(hd * X_HEAD_DIM, (hd + 1) * X_HEAD_DIM)
        s = lax.dot_general(q[:, lanes], k_ref[:, lanes], (((1,), (1,)), ((), ())),
                            preferred_element_type=F32)
        e = jnp.exp(s - jnp.max(s, axis=-1, keepdims=True))
        p = e * (1.0 / jnp.sum(e, axis=-1, keepdims=True))
        outs.append(_dot(p.astype(BF16), v_ref[:, lanes]).astype(BF16))
    o = jnp.concatenate(outs, axis=1)
    o_ref[...] = x + _dot(o, wco_ref[...])


def _mix_out(l, x2d, y_ssm, y_ret, w_out, g, wq, k_mem, v_mem, wco, tiles_per_batch):
    n = x2d.shape[0]
    row = lambda w: pl.BlockSpec((TM, w), lambda i: (i, 0))
    mem = pl.BlockSpec((None, MEM_LEN, D_MODEL), lambda i: (l, i // tiles_per_batch, 0))
    sq = _layer_spec(l, (D_MODEL, D_MODEL))
    return pl.pallas_call(
        _mix_out_kernel,
        grid=(n // TM,),
        in_specs=[row(D_MODEL), _blocked_spec(tiles_per_batch),
                  row(RET_WIDTH), sq, _layer_spec(l, (1, D_MODEL)), sq, mem, mem, sq],
        out_specs=row(D_MODEL),
        out_shape=jax.ShapeDtypeStruct((n, D_MODEL), F32),
        scratch_shapes=[pltpu.VMEM((D_MODEL // LANES, TM, LANES), F32)],
        compiler_params=_params("arbitrary"),
        name="mix_out_xattn",
    )(x2d, y_ssm, y_ret, w_out, g, wq, k_mem, v_mem, wco)


def _ffn_kernel(final, x_ref, g_ref, wg_ref, wu_ref, wd_ref, gf_ref, o_ref):
    x = x_ref[...]
    h = _rms(x, g_ref[...]).astype(BF16)
    acc = x
    for lo, hi in FF_SPLITS:
        act = jax.nn.silu(_dot(h, wg_ref[:, lo:hi])) * _dot(h, wu_ref[:, lo:hi])
        acc = acc + _dot(act.astype(BF16), wd_ref[lo:hi, :])
    if final:
        acc = _rms(acc, gf_ref[...])
    o_ref[...] = acc


def _ffn(l, x2d, g, wg, wu, wd, g_final, final):
    n = x2d.shape[0]
    row = pl.BlockSpec((TM, D_MODEL), lambda i: (i, 0))
    return pl.pallas_call(
        functools.partial(_ffn_kernel, final),
        grid=(n // TM,),
        in_specs=[row, _layer_spec(l, (1, D_MODEL)), _layer_spec(l, (D_MODEL, D_FF)),
                  _layer_spec(l, (D_MODEL, D_FF)), _layer_spec(l, (D_FF, D_MODEL)), _const_spec((1, D_MODEL))],
        out_specs=row,
        out_shape=jax.ShapeDtypeStruct((n, D_MODEL), F32),
        compiler_params=_params("arbitrary"),
        name="ffn_final" if final else "ffn",
    )(x2d, g, wg, wu, wd, g_final.reshape(1, D_MODEL))


def kernel(x, mem, positions, norm_mix, w_in, s5_lambda_re, s5_lambda_im, s5_log_step, s5_b_re, s5_b_im, s5_c_re, s5_c_im, s5_d, s5_w_glu, s5_b_glu, s5_out_norm, ret_out_norm, w_out, norm_cross, norm_mem, w_cq, w_ck, w_cv, w_co, norm_ffn, w_gate, w_up, w_down, norm_final):
    nb, seq, _ = x.shape
    n = nb * seq
    assert seq % RET_TR == 0 and seq % (S5_R * S5_TK) == 0 and seq % TM == 0
    bf = lambda a: a.astype(BF16)
    vec = lambda a: a.reshape(DEPTH, 1, a.shape[-1])

    cos2, sin2 = _rope_tables(positions)
    a_r, a_i, s5_bp, s5_w = _s5_discretise(
        s5_lambda_re, s5_lambda_im, s5_log_step, s5_b_re, s5_b_im, s5_c_re, s5_c_im)
    a_r = a_r.reshape(DEPTH, S5_KT, 1, S5_SLAB)
    a_i = a_i.reshape(DEPTH, S5_KT, 1, S5_SLAB)
    k_mem, v_mem = _mem_kv(mem.reshape(nb * MEM_LEN, D_MODEL), norm_mem, bf(w_ck), bf(w_cv))
    lg = jnp.log1p(-jnp.exp2(-5.0 - jnp.arange(RET_HEADS, dtype=F32)))
    lg = jnp.broadcast_to(lg[:, None, None], (RET_HEADS, 1, RET_DK))
    ret_gain = ret_out_norm.reshape(DEPTH, RET_HEADS, 1, RET_DK)
    w_in_b, w_out_b, w_cq_b, w_co_b = bf(w_in), bf(w_out), bf(w_cq * X_HEAD_DIM ** -0.5), bf(w_co)
    w_gate_b, w_up_b, w_down_b, w_glu_b = bf(w_gate), bf(w_up), bf(w_down), bf(s5_w_glu)
    g_mix, g_cross, g_ffn = vec(norm_mix), vec(norm_cross), vec(norm_ffn)
    s5_d3, s5_bglu3, s5_gout3 = vec(s5_d), vec(s5_b_glu), vec(s5_out_norm)

    x2d = x.reshape(n, D_MODEL)
    for l in range(DEPTH):
        u, ub, q, k, v, gate = _in_proj(l, x2d, g_mix, w_in_b, cos2, sin2, nb, seq)
        y_ssm = _s5_mixer(l, u, ub, s5_bp, a_r, a_i, s5_w,
                          s5_d3, w_glu_b, s5_bglu3, s5_gout3)
        y_ret = _retention(l, q, k, v, gate, lg, ret_gain, nb, seq)
        x2d = _mix_out(l, x2d, y_ssm, y_ret, w_out_b, g_cross, w_cq_b,
                       k_mem, v_mem, w_co_b, seq // TM)
        x2d = _ffn(l, x2d, g_ffn, w_gate_b, w_up_b, w_down_b, norm_final, l == DEPTH - 1)
    return x2d.reshape(nb, seq, D_MODEL)
```

```python
import functools
import math

import jax
import jax.numpy as jnp
from jax import lax
from jax.experimental import pallas as pl
from jax.experimental.pallas import tpu as pltpu

D_MODEL = 1024
DEPTH = 4
MEM_LEN = 256
S5_WIDTH = 512
S5_GROUP_CH = 16
S5_GROUPS = 32
S5_STATE = 64
RET_HEADS = 4
RET_DK = 128
RET_WIDTH = 512
IN_WIDTH = S5_WIDTH + 4 * RET_WIDTH
X_HEADS = 4
X_HEAD_DIM = 256
D_FF = 2816
ROPE_BASE = 10000.0
EPS = 1e-6

LANES = 128
SUBLANES = 8
S5_KT = S5_WIDTH // LANES
S5_GPT = LANES // S5_GROUP_CH
S5_SLAB = S5_GPT * S5_STATE
MXU_DIM = 256
S5_HG = MXU_DIM // S5_STATE
S5_KT_HALVES = S5_GPT // S5_HG
S5_R = 4
S5_BP_SHAPE = (S5_KT, S5_KT_HALVES, S5_R * S5_HG * S5_GROUP_CH, 2 * S5_HG * S5_STATE)
S5_W_SHAPE = (S5_KT, S5_KT_HALVES, (2 * S5_STATE + S5_R * S5_GROUP_CH) * S5_HG,
              S5_R * S5_HG * S5_GROUP_CH)
S5_TK = 128
S5_PITCH = S5_TK + 8
RET_CHUNK = 256
RET_TR = 1024
TM = 1024
FF_SPLITS = ((0, 1536), (1536, D_FF))

BF16 = jnp.bfloat16
F32 = jnp.float32
VMEM_LIMIT = 48 * 1024 * 1024


def _dot(a, b):
    return jnp.dot(a, b, preferred_element_type=F32)


def _rms(x, g):
    ms = jnp.mean(x * x, axis=-1, keepdims=True)
    return x * lax.rsqrt(ms + EPS) * g


def _const_spec(shape):
    nd = len(shape)
    return pl.BlockSpec(shape, lambda *_: (0,) * nd, pipeline_mode=pl.Buffered(1))


def _layer_spec(l, shape):
    nd = len(shape)
    return pl.BlockSpec((None,) + tuple(shape), lambda *_: (l,) + (0,) * nd, pipeline_mode=pl.Buffered(1))


def _params(*sem):
    return pltpu.CompilerParams(dimension_semantics=sem, vmem_limit_bytes=VMEM_LIMIT)


def _rope_kernel(pos_ref, invf_ref, sign_ref, c_ref, s_ref):
    ang = pos_ref[...] * invf_ref[...]
    c_ref[...] = jnp.cos(ang)
    s_ref[...] = jnp.sin(ang) * sign_ref[...]


def _rope_tables(positions):
    n = positions.size
    half = RET_DK // 2
    inv_freq = 1.0 / (ROPE_BASE ** (jnp.arange(half, dtype=F32) / half))
    invf2 = jnp.concatenate([inv_freq, inv_freq]).reshape(1, RET_DK)
    sign = jnp.concatenate([-jnp.ones((half,), F32), jnp.ones((half,), F32)]).reshape(1, RET_DK)
    pos = positions.astype(F32).reshape(n, 1)
    t = 1024
    return pl.pallas_call(
        _rope_kernel,
        grid=(n // t,),
        in_specs=[pl.BlockSpec((t, 1), lambda i: (i, 0)),
                  pl.BlockSpec((1, RET_DK), lambda i: (0, 0)),
                  pl.BlockSpec((1, RET_DK), lambda i: (0, 0))],
        out_specs=[pl.BlockSpec((t, RET_DK), lambda i: (i, 0))] * 2,
        out_shape=[jax.ShapeDtypeStruct((n, RET_DK), F32)] * 2,
        compiler_params=_params("arbitrary"),
        name="rope_tables",
    )(pos, invf2, sign)


def _tile_lanes(x, reps):
    while reps > 1:
        x = jnp.concatenate([x, x], axis=1)
        reps //= 2
    return x


def _block_diag(blocks, kt):
    _, r, c = blocks.shape
    x = _tile_lanes(blocks[kt * S5_GPT:(kt + 1) * S5_GPT].reshape(S5_GPT * r, c), S5_GPT)
    row_group = lax.broadcasted_iota(jnp.int32, x.shape, 0) // r
    col_group = lax.broadcasted_iota(jnp.int32, x.shape, 1) // c
    return jnp.where(row_group == col_group, x, 0.0)


def _s5_disc_kernel(lr_ref, li_ref, ls_ref, brt_ref, bit_ref, cr_ref, ci_ref, ar_ref, ai_ref, bp_ref, w_ref):
    lr = lr_ref[0]
    li = li_ref[0]
    step = jnp.exp(ls_ref[0])
    mag = jnp.exp(lr * step)
    a_r = mag * jnp.cos(li * step)
    a_i = mag * jnp.sin(li * step)
    den = lr * lr + li * li
    f_r = ((a_r - 1.0) * lr + a_i * li) / den
    f_i = (a_i * lr - (a_r - 1.0) * li) / den
    br = brt_ref[0]
    bi = bit_ref[0]
    bb_r = f_r * br - f_i * bi
    bb_i = f_r * bi + f_i * br
    cr = cr_ref[0]
    ci = ci_ref[0]
    contract_p = (((2,), (2,)), ((0,), (0,)))
    hi = dict(precision=lax.Precision.HIGHEST, preferred_element_type=F32)
    hc, hs = S5_HG * S5_GROUP_CH, S5_HG * S5_STATE
    pw_r = jnp.ones_like(a_r)
    pw_i = jnp.zeros_like(a_i)
    ro_r, ro_i, lags = [], [], []
    for m in range(S5_R):
        abb_r = pw_r * bb_r - pw_i * bb_i
        abb_i = pw_r * bb_i + pw_i * bb_r
        lag = lax.dot_general(abb_r, cr, contract_p, **hi) - lax.dot_general(abb_i, ci, contract_p, **hi)
        pw_r, pw_i = pw_r * a_r - pw_i * a_i, pw_r * a_i + pw_i * a_r
        cp_r = cr * pw_r - ci * pw_i
        cp_i = cr * pw_i + ci * pw_r
        i = S5_R - 1 - m
        for kt in range(S5_KT):
            bd_r = _block_diag(abb_r, kt)
            bd_i = _block_diag(abb_i, kt)
            for half in range(S5_KT_HALVES):
                rs, cs = slice(half * hc, (half + 1) * hc), slice(half * hs, (half + 1) * hs)
                bp_ref[0, kt, half, i * hc:(i + 1) * hc, :] = jnp.concatenate(
                    [bd_r[rs, cs], bd_i[rs, cs]], axis=1).astype(BF16)
        ro_r.append([_block_diag(cp_r, kt).T for kt in range(S5_KT)])
        ro_i.append([(-_block_diag(cp_i, kt)).T for kt in range(S5_KT)])
        lags.append([_block_diag(lag, kt) for kt in range(S5_KT)])
    zero = jnp.zeros((hc, hc), F32)
    for kt in range(S5_KT):
        for half in range(S5_KT_HALVES):
            rs, cs = slice(half * hs, (half + 1) * hs), slice(half * hc, (half + 1) * hc)
            w_ref[0, kt, half, 0:hs, :] = jnp.concatenate(
                [ro_r[j][kt][rs, cs] for j in range(S5_R)], axis=1).astype(BF16)
            w_ref[0, kt, half, hs:2 * hs, :] = jnp.concatenate(
                [ro_i[j][kt][rs, cs] for j in range(S5_R)], axis=1).astype(BF16)
            for i in range(S5_R):
                w_ref[0, kt, half, 2 * hs + i * hc:2 * hs + (i + 1) * hc, :] = jnp.concatenate(
                    [lags[j - i][kt][cs, cs] if j >= i else zero for j in range(S5_R)], axis=1).astype(BF16)
    ar_ref[0] = pw_r
    ai_ref[0] = pw_i


def _s5_discretise(lam_re, lam_im, log_step, b_re, b_im, c_re, c_im):
    g, p, h = S5_GROUPS, S5_STATE, S5_GROUP_CH
    lr = lam_re.reshape(DEPTH, g, 1, p)
    li = lam_im.reshape(DEPTH, g, 1, p)
    ls = log_step.reshape(DEPTH, g, 1, 1)
    brt = jnp.swapaxes(b_re, -1, -2)
    bit = jnp.swapaxes(b_im, -1, -2)
    vec = pl.BlockSpec((1, g, 1, p), lambda l: (l, 0, 0, 0))
    mat = pl.BlockSpec((1, g, h, p), lambda l: (l, 0, 0, 0))
    return pl.pallas_call(
        _s5_disc_kernel,
        grid=(DEPTH,),
        in_specs=[vec, vec, pl.BlockSpec((1, g, 1, 1), lambda l: (l, 0, 0, 0)), mat, mat, mat, mat],
        out_specs=[vec, vec, pl.BlockSpec((1,) + S5_BP_SHAPE, lambda l: (l, 0, 0, 0, 0)),
                   pl.BlockSpec((1,) + S5_W_SHAPE, lambda l: (l, 0, 0, 0, 0))],
        out_shape=[jax.ShapeDtypeStruct((DEPTH, g, 1, p), F32)] * 2
        + [jax.ShapeDtypeStruct((DEPTH,) + S5_BP_SHAPE, BF16), jax.ShapeDtypeStruct((DEPTH,) + S5_W_SHAPE, BF16)],
        compiler_params=_params("arbitrary"),
        name="s5_discretise",
    )(lr, li, ls, brt, bit, c_re, c_im)


def _memkv_kernel(mem_ref, g_ref, wk_ref, wv_ref, k_ref, v_ref):
    m = _rms(mem_ref[...], g_ref[0]).astype(BF16)
    k_ref[0] = _dot(m, wk_ref[0]).astype(BF16)
    v_ref[0] = _dot(m, wv_ref[0]).astype(BF16)


def _mem_kv(mem2d, norm_mem, w_ck, w_cv):
    rows = mem2d.shape[0]
    wspec = pl.BlockSpec((1, D_MODEL, D_MODEL), lambda l: (l, 0, 0))
    ospec = pl.BlockSpec((1, rows, D_MODEL), lambda l: (l, 0, 0))
    return pl.pallas_call(
        _memkv_kernel,
        grid=(DEPTH,),
        in_specs=[pl.BlockSpec((rows, D_MODEL), lambda l: (0, 0)),
                  pl.BlockSpec((1, 1, D_MODEL), lambda l: (l, 0, 0)), wspec, wspec],
        out_specs=[ospec, ospec],
        out_shape=[jax.ShapeDtypeStruct((DEPTH, rows, D_MODEL), BF16)] * 2,
        compiler_params=_params("arbitrary"),
        name="mem_kv",
    )(mem2d, norm_mem.reshape(DEPTH, 1, D_MODEL), w_ck, w_cv)


def _in_proj_kernel(x_ref, g_ref, w_ref, c_ref, s_ref, u_ref, ub_ref, q_ref, k_ref, v_ref, gate_ref, us_ref):
    h = _rms(x_ref[...], g_ref[...]).astype(BF16)
    cos = c_ref[...]
    sin = s_ref[...]
    u = _dot(h, w_ref[:, 0:S5_WIDTH])
    for c in range(S5_KT):
        us_ref[c] = u[:, c * LANES:(c + 1) * LANES]
    hc = S5_HG * S5_GROUP_CH
    for c in range(S5_KT):
        pieces = [us_ref[c, pl.ds(i, TM // S5_R, stride=S5_R), :] for i in range(S5_R)]
        for i in range(S5_R):
            u_ref[:, i * S5_WIDTH + c * LANES:i * S5_WIDTH + (c + 1) * LANES] = pieces[i]
        for half in range(S5_KT_HALVES):
            cols = jnp.concatenate([p[:, half * hc:(half + 1) * hc] for p in pieces], axis=1)
            base = (c * S5_KT_HALVES + half) * S5_R * hc
            ub_ref[:, base:base + S5_R * hc] = cols.astype(BF16)
    for dst, off, scale in ((q_ref, S5_WIDTH, None), (k_ref, S5_WIDTH + RET_WIDTH, RET_DK ** -0.5)):
        t = _dot(h, w_ref[:, off:off + RET_WIDTH])
        for hd in range(RET_HEADS):
            th = t[:, hd * RET_DK:(hd + 1) * RET_DK]
            r = th * cos + pltpu.roll(th, RET_DK // 2, axis=1) * sin
            if scale is not None:
                r = r * scale
            dst[:, hd * RET_DK:(hd + 1) * RET_DK] = r.astype(BF16)
    off = S5_WIDTH + 2 * RET_WIDTH
    v_ref[...] = _dot(h, w_ref[:, off:off + RET_WIDTH]).astype(BF16)
    gate_ref[...] = _dot(h, w_ref[:, off + RET_WIDTH:off + 2 * RET_WIDTH])


def _blocked_spec(tiles_per_batch):
    return pl.BlockSpec((None, TM // S5_R, S5_R * S5_WIDTH),
                        lambda i: (i // tiles_per_batch, i % tiles_per_batch, 0))


def _in_proj(l, x2d, g, w_bf16, cos2, sin2, nb, seq):
    n = x2d.shape[0]
    row = lambda w: pl.BlockSpec((TM, w), lambda i: (i, 0))
    ret_bf16 = jax.ShapeDtypeStruct((n, RET_WIDTH), BF16)
    return pl.pallas_call(
        _in_proj_kernel,
        grid=(n // TM,),
        in_specs=[row(D_MODEL), _layer_spec(l, (1, D_MODEL)), _layer_spec(l, (D_MODEL, IN_WIDTH)),
                  row(RET_DK), row(RET_DK)],
        out_specs=[_blocked_spec(seq // TM), _blocked_spec(seq // TM)] + [row(RET_WIDTH)] * 4,
        out_shape=[jax.ShapeDtypeStruct((nb, seq // S5_R, S5_R * S5_WIDTH), F32),
                   jax.ShapeDtypeStruct((nb, seq // S5_R, S5_R * S5_WIDTH), BF16),
                   ret_bf16, ret_bf16, ret_bf16,
                   jax.ShapeDtypeStruct((n, RET_WIDTH), F32)],
        scratch_shapes=[pltpu.VMEM((S5_KT, TM, LANES), F32)],
        compiler_params=_params("arbitrary"),
        name="in_proj",
    )(x2d, g, w_bf16, cos2, sin2)


def _s5_kernel(nb, u_ref, ub_ref, bp_ref, ar_ref, ai_ref, w_ref, d_ref, wglu_ref, bglu_ref, gout_ref,
               o_ref, st_ref, y_ref, *xs_refs):
    nslab = S5_SLAB // LANES
    rows = nb * S5_TK

    @pl.when(pl.program_id(0) == 0)
    def _():
        st_ref[...] = jnp.zeros_like(st_ref)

    pack = SUBLANES // nb
    npk = nslab // pack
    hc, hs = S5_HG * S5_GROUP_CH, S5_HG * S5_STATE
    half_slabs = hs // LANES

    def slab_block(s, b):
        return s // pack, ((s % pack) * nb + b) * S5_PITCH

    def half_state_slabs(half):
        re = [half * half_slabs + j for j in range(half_slabs)]
        return re + [nslab + s for s in re]

    def block_inputs(kt, half):
        base = (kt * S5_KT_HALVES + half) * S5_R * hc
        return ub_ref[:, :, base:base + S5_R * hc].reshape(rows, S5_R * hc)

    def state_input(kt, half):
        xb = _dot(block_inputs(kt, half), bp_ref[kt, half])
        for b in range(nb):
            for jj, s in enumerate(half_state_slabs(half)):
                p, r0 = slab_block(s, b)
                xs_refs[kt][p, r0:r0 + S5_TK, :] = xb[b * S5_TK:(b + 1) * S5_TK, jj * LANES:(jj + 1) * LANES]

    def readout(kt):
        def slab(s, b):
            p, r0 = slab_block(s, b)
            return xs_refs[kt][p, r0:r0 + S5_TK, :]

        yk = []
        for half in range(S5_KT_HALVES):
            s_prev = jnp.concatenate(
                [jnp.concatenate([slab(s, b) for s in half_state_slabs(half)], axis=1) for b in range(nb)],
                axis=0).astype(BF16)
            yk.append(_dot(jnp.concatenate([s_prev, block_inputs(kt, half)], axis=1), w_ref[kt, half]))
        for j in range(S5_R):
            at = slice(j * S5_WIDTH + kt * LANES, j * S5_WIDTH + (kt + 1) * LANES)
            y = (jnp.concatenate([h[:, j * hc:(j + 1) * hc] for h in yk], axis=1).reshape(nb, S5_TK, LANES)
                 + d_ref[:, kt * LANES:(kt + 1) * LANES] * u_ref[:, :, at])
            y_ref[:, :, at] = 0.5 * y * (1.0 + lax.erf(y * math.sqrt(0.5)))

    def recurrence(kt):
        def packed(ref, q):
            return jnp.concatenate(
                [jnp.broadcast_to(ref[kt, :, s * LANES:(s + 1) * LANES], (nb, LANES))
                 for s in range(q * pack, (q + 1) * pack)], axis=0)

        a_r = [packed(ar_ref, q) for q in range(npk)]
        a_i = [packed(ai_ref, q) for q in range(npk)]
        s_r = [st_ref[kt, q] for q in range(npk)]
        s_i = [st_ref[kt, npk + q] for q in range(npk)]
        for t in range(S5_TK):
            at = pl.ds(t, pack * nb, stride=S5_PITCH)
            for q in range(npk):
                x_r = xs_refs[kt][q, at, :]
                x_i = xs_refs[kt][npk + q, at, :]
                xs_refs[kt][q, at, :] = s_r[q]
                xs_refs[kt][npk + q, at, :] = s_i[q]
                s_r[q], s_i[q] = (a_r[q] * s_r[q] - a_i[q] * s_i[q] + x_r,
                                  a_r[q] * s_i[q] + a_i[q] * s_r[q] + x_i)
        for q in range(npk):
            st_ref[kt, q] = s_r[q]
            st_ref[kt, npk + q] = s_i[q]

    for kt in range(S5_KT):
        for half in range(S5_KT_HALVES):
            state_input(kt, half)
    for kt in range(S5_KT):
        recurrence(kt)
        readout(kt)

    for i in range(S5_R):
        cols = slice(i * S5_WIDTH, (i + 1) * S5_WIDTH)
        z = y_ref[:, :, cols].reshape(rows, S5_WIDTH)
        z = z * jax.nn.sigmoid(_dot(z.astype(BF16), wglu_ref[...]) + bglu_ref[...])
        o_ref[:, :, cols] = _rms(z, gout_ref[...]).astype(BF16).reshape(nb, S5_TK, S5_WIDTH)


def _s5_mixer(l, u3, ub3, bp, a_r, a_i, w, d, wglu, bglu, gout):
    nb, krows, width = u3.shape
    assert SUBLANES % nb == 0
    nbuf = 2 * S5_SLAB // LANES * nb // SUBLANES
    tok = pl.BlockSpec((nb, S5_TK, width), lambda c: (0, c, 0))
    return pl.pallas_call(
        functools.partial(_s5_kernel, nb),
        grid=(krows // S5_TK,),
        in_specs=[tok, tok,
                  _layer_spec(l, S5_BP_SHAPE),
                  _layer_spec(l, (S5_KT, 1, S5_SLAB)), _layer_spec(l, (S5_KT, 1, S5_SLAB)),
                  _layer_spec(l, S5_W_SHAPE),
                  _layer_spec(l, (1, S5_WIDTH)), _layer_spec(l, (S5_WIDTH, S5_WIDTH)),
                  _layer_spec(l, (1, S5_WIDTH)), _layer_spec(l, (1, S5_WIDTH))],
        out_specs=tok,
        out_shape=jax.ShapeDtypeStruct((nb, krows, width), BF16),
        scratch_shapes=[pltpu.VMEM((S5_KT, nbuf, SUBLANES, LANES), F32),
                        pltpu.VMEM((nb, S5_TK, width), F32)]
        + [pltpu.VMEM((nbuf, SUBLANES * S5_PITCH, LANES), F32)] * S5_KT,
        compiler_params=_params("arbitrary"),
        name="s5_mixer",
    )(u3, ub3, bp, a_r, a_i, w, d, wglu, bglu, gout)


def _ret_kernel(q_ref, k_ref, v_ref, g_ref, lg_ref, gain_ref, o_ref, st_ref, dm_ref, zeta_ref, xi_ref):
    @pl.when((pl.program_id(0) == 0) & (pl.program_id(1) == 0))
    def _():
        row = lax.broadcasted_iota(jnp.int32, (RET_CHUNK, RET_CHUNK), 0).astype(F32)
        col = lax.broadcasted_iota(jnp.int32, (RET_CHUNK, RET_CHUNK), 1).astype(F32)
        diff = row - col
        idx = lax.broadcasted_iota(jnp.int32, (RET_CHUNK, RET_DK), 0).astype(F32)
        for hd in range(RET_HEADS):
            lg = lg_ref[hd]
            lg_wide = jnp.concatenate([lg] * (RET_CHUNK // RET_DK), axis=1)
            dm_ref[hd] = jnp.where(diff >= 0, jnp.exp(jnp.maximum(diff, 0.0) * lg_wide), 0.0)
            zeta_ref[hd] = jnp.exp((RET_CHUNK - 1.0 - idx) * lg)
            xi_ref[hd] = jnp.exp((idx + 1.0) * lg)

    @pl.when(pl.program_id(1) == 0)
    def _():
        st_ref[...] = jnp.zeros_like(st_ref)

    for c in range(RET_TR // RET_CHUNK):
        rows = slice(c * RET_CHUNK, (c + 1) * RET_CHUNK)
        for hd in range(RET_HEADS):
            lanes = slice(hd * RET_DK, (hd + 1) * RET_DK)
            qh = q_ref[rows, lanes]
            kh = k_ref[rows, lanes]
            vh = v_ref[rows, lanes]
            s = lax.dot_general(qh, kh, (((1,), (1,)), ((), ())), preferred_element_type=F32)
            inner = _dot((s * dm_ref[hd]).astype(BF16), vh)
            state = st_ref[hd]
            cross = _dot(qh, state.astype(BF16)) * xi_ref[hd]
            kz = (kh.astype(F32) * zeta_ref[hd]).astype(BF16)
            kv = lax.dot_general(kz, vh, (((0,), (0,)), ((), ())), preferred_element_type=F32)
            st_ref[hd] = jnp.exp(RET_CHUNK * lg_ref[hd]) * state + kv
            o = inner + cross
            mu = jnp.mean(o, axis=-1, keepdims=True)
            oc = o - mu
            var = jnp.mean(oc * oc, axis=-1, keepdims=True)
            o = oc * lax.rsqrt(var + EPS) * gain_ref[hd]
            o_ref[rows, lanes] = (o * jax.nn.silu(g_ref[rows, lanes])).astype(BF16)


def _retention(l, q, k, v, gate, lg, gain, nb, seq):
    steps = seq // RET_TR
    tok = pl.BlockSpec((RET_TR, RET_WIDTH), lambda b, c: (b * steps + c, 0))
    return pl.pallas_call(
        _ret_kernel,
        grid=(nb, steps),
        in_specs=[tok, tok, tok, tok, _const_spec((RET_HEADS, 1, RET_DK)),
                  _layer_spec(l, (RET_HEADS, 1, RET_DK))],
        out_specs=tok,
        out_shape=jax.ShapeDtypeStruct((nb * seq, RET_WIDTH), BF16),
        scratch_shapes=[pltpu.VMEM((RET_HEADS, RET_DK, RET_DK), F32),
                        pltpu.VMEM((RET_HEADS, RET_CHUNK, RET_CHUNK), F32),
                        pltpu.VMEM((RET_HEADS, RET_CHUNK, RET_DK), F32),
                        pltpu.VMEM((RET_HEADS, RET_CHUNK, RET_DK), F32)],
        compiler_params=_params("arbitrary", "arbitrary"),
        name="retention",
    )(q, k, v, gate, lg, gain)


def _mix_out_kernel(x_ref, ys_ref, yr_ref, wo_ref, g_ref, wq_ref, k_ref, v_ref, wco_ref, o_ref, ps_ref):
    tk = TM // S5_R
    ys = jnp.concatenate([ys_ref[:, i * S5_WIDTH:(i + 1) * S5_WIDTH] for i in range(S5_R)], axis=0)
    ps = _dot(ys, wo_ref[0:S5_WIDTH, :])
    for i in range(S5_R):
        for c in range(D_MODEL // LANES):
            ps_ref[c, pl.ds(i, tk, stride=S5_R), :] = ps[i * tk:(i + 1) * tk, c * LANES:(c + 1) * LANES]
    ssm = jnp.concatenate([ps_ref[c] for c in range(D_MODEL // LANES)], axis=1)
    x = x_ref[...] + ssm + _dot(yr_ref[...], wo_ref[S5_WIDTH:, :])
    h = _rms(x, g_ref[...]).astype(BF16)
    q = _dot(h, wq_ref[...]).astype(BF16)
    outs = []
    for hd in range(X_HEADS):
        lanes = slice(hd * X_HEAD_DIM, (hd + 1) * X_HEAD_DIM)
        s = lax.dot_general(q[:, lanes], k_ref[:, lanes], (((1,), (1,)), ((), ())),
                            preferred_element_type=F32)
        e = jnp.exp(s - jnp.max(s, axis=-1, keepdims=True))
        p = e * (1.0 / jnp.sum(e, axis=-1, keepdims=True))
        outs.append(_dot(p.astype(BF16), v_ref[:, lanes]).astype(BF16))
    o = jnp.concatenate(outs, axis=1)
    o_ref[...] = x + _dot(o, wco_ref[...])


def _mix_out(l, x2d, y_ssm, y_ret, w_out, g, wq, k_mem, v_mem, wco, tiles_per_batch):
    n = x2d.shape[0]
    row = lambda w: pl.BlockSpec((TM, w), lambda i: (i, 0))
    mem = pl.BlockSpec((None, MEM_LEN, D_MODEL), lambda i: (l, i // tiles_per_batch, 0))
    sq = _layer_spec(l, (D_MODEL, D_MODEL))
    return pl.pallas_call(
        _mix_out_kernel,
        grid=(n // TM,),
        in_specs=[row(D_MODEL), _blocked_spec(tiles_per_batch),
                  row(RET_WIDTH), sq, _layer_spec(l, (1, D_MODEL)), sq, mem, mem, sq],
        out_specs=row(D_MODEL),
        out_shape=jax.ShapeDtypeStruct((n, D_MODEL), F32),
        scratch_shapes=[pltpu.VMEM((D_MODEL // LANES, TM, LANES), F32)],
        compiler_params=_params("arbitrary"),
        name="mix_out_xattn",
    )(x2d, y_ssm, y_ret, w_out, g, wq, k_mem, v_mem, wco)


def _ffn_kernel(final, x_ref, g_ref, wg_ref, wu_ref, wd_ref, gf_ref, o_ref):
    x = x_ref[...]
    h = _rms(x, g_ref[...]).astype(BF16)
    acc = x
    for lo, hi in FF_SPLITS:
        act = jax.nn.silu(_dot(h, wg_ref[:, lo:hi])) * _dot(h, wu_ref[:, lo:hi])
        acc = acc + _dot(act.astype(BF16), wd_ref[lo:hi, :])
    if final:
        acc = _rms(acc, gf_ref[...])
    o_ref[...] = acc


def _ffn(l, x2d, g, wg, wu, wd, g_final, final):
    n = x2d.shape[0]
    row = pl.BlockSpec((TM, D_MODEL), lambda i: (i, 0))
    return pl.pallas_call(
        functools.partial(_ffn_kernel, final),
        grid=(n // TM,),
        in_specs=[row, _layer_spec(l, (1, D_MODEL)), _layer_spec(l, (D_MODEL, D_FF)),
                  _layer_spec(l, (D_MODEL, D_FF)), _layer_spec(l, (D_FF, D_MODEL)), _const_spec((1, D_MODEL))],
        out_specs=row,
        out_shape=jax.ShapeDtypeStruct((n, D_MODEL), F32),
        compiler_params=_params("arbitrary"),
        name="ffn_final" if final else "ffn",
    )(x2d, g, wg, wu, wd, g_final.reshape(1, D_MODEL))


def kernel(x, mem, positions, norm_mix, w_in, s5_lambda_re, s5_lambda_im, s5_log_step, s5_b_re, s5_b_im, s5_c_re, s5_c_im, s5_d, s5_w_glu, s5_b_glu, s5_out_norm, ret_out_norm, w_out, norm_cross, norm_mem, w_cq, w_ck, w_cv, w_co, norm_ffn, w_gate, w_up, w_down, norm_final):
    nb, seq, _ = x.shape
    n = nb * seq
    assert seq % RET_TR == 0 and seq % (S5_R * S5_TK) == 0 and seq % TM == 0
    bf = lambda a: a.astype(BF16)
    vec = lambda a: a.reshape(DEPTH, 1, a.shape[-1])

    cos2, sin2 = _rope_tables(positions)
    a_r, a_i, s5_bp, s5_w = _s5_discretise(
        s5_lambda_re, s5_lambda_im, s5_log_step, s5_b_re, s5_b_im, s5_c_re, s5_c_im)
    a_r = a_r.reshape(DEPTH, S5_KT, 1, S5_SLAB)
    a_i = a_i.reshape(DEPTH, S5_KT, 1, S5_SLAB)
    k_mem, v_mem = _mem_kv(mem.reshape(nb * MEM_LEN, D_MODEL), norm_mem, bf(w_ck), bf(w_cv))
    lg = jnp.log1p(-jnp.exp2(-5.0 - jnp.arange(RET_HEADS, dtype=F32)))
    lg = jnp.broadcast_to(lg[:, None, None], (RET_HEADS, 1, RET_DK))
    ret_gain = ret_out_norm.reshape(DEPTH, RET_HEADS, 1, RET_DK)
    w_in_b, w_out_b, w_cq_b, w_co_b = bf(w_in), bf(w_out), bf(w_cq * X_HEAD_DIM ** -0.5), bf(w_co)
    w_gate_b, w_up_b, w_down_b, w_glu_b = bf(w_gate), bf(w_up), bf(w_down), bf(s5_w_glu)
    g_mix, g_cross, g_ffn = vec(norm_mix), vec(norm_cross), vec(norm_ffn)
    s5_d3, s5_bglu3, s5_gout3 = vec(s5_d), vec(s5_b_glu), vec(s5_out_norm)

    x2d = x.reshape(n, D_MODEL)
    for l in range(DEPTH):
        u, ub, q, k, v, gate = _in_proj(l, x2d, g_mix, w_in_b, cos2, sin2, nb, seq)
        y_ssm = _s5_mixer(l, u, ub, s5_bp, a_r, a_i, s5_w,
                          s5_d3, w_glu_b, s5_bglu3, s5_gout3)
        y_ret = _retention(l, q, k, v, gate, lg, ret_gain, nb, seq)
        x2d = _mix_out(l, x2d, y_ssm, y_ret, w_out_b, g_cross, w_cq_b,
                       k_mem, v_mem, w_co_b, seq // TM)
        x2d = _ffn(l, x2d, g_ffn, w_gate_b, w_up_b, w_down_b, norm_final, l == DEPTH - 1)
    return x2d.reshape(nb, seq, D_MODEL)
```

```python
import functools
import math

import jax
import jax.numpy as jnp
from jax import lax
from jax.experimental import pallas as pl
from jax.experimental.pallas import tpu as pltpu

D_MODEL = 1024
DEPTH = 4
MEM_LEN = 256
S5_WIDTH = 512
S5_GROUP_CH = 16
S5_GROUPS = 32
S5_STATE = 64
RET_HEADS = 4
RET_DK = 128
RET_WIDTH = 512
IN_WIDTH = S5_WIDTH + 4 * RET_WIDTH
X_HEADS = 4
X_HEAD_DIM = 256
D_FF = 2816
ROPE_BASE = 10000.0
EPS = 1e-6

LANES = 128
SUBLANES = 8
S5_KT = S5_WIDTH // LANES
S5_GPT = LANES // S5_GROUP_CH
S5_SLAB = S5_GPT * S5_STATE
MXU_DIM = 256
S5_HG = MXU_DIM // S5_STATE
S5_KT_HALVES = S5_GPT // S5_HG
S5_R = 4
S5_BP_SHAPE = (S5_KT, S5_KT_HALVES, S5_R * S5_HG * S5_GROUP_CH, 2 * S5_HG * S5_STATE)
S5_W_SHAPE = (S5_KT, S5_KT_HALVES, (2 * S5_STATE + S5_R * S5_GROUP_CH) * S5_HG,
              S5_R * S5_HG * S5_GROUP_CH)
S5_TK = 128
S5_PITCH = S5_TK + 8
RET_CHUNK = 256
RET_TR = 2048
TM = 1024
FF_SPLITS = ((0, 1536), (1536, D_FF))

BF16 = jnp.bfloat16
F32 = jnp.float32
VMEM_LIMIT = 48 * 1024 * 1024


def _dot(a, b):
    return jnp.dot(a, b, preferred_element_type=F32)


def _rms(x, g):
    ms = jnp.mean(x * x, axis=-1, keepdims=True)
    return x * lax.rsqrt(ms + EPS) * g


def _const_spec(shape):
    nd = len(shape)
    return pl.BlockSpec(shape, lambda *_: (0,) * nd, pipeline_mode=pl.Buffered(1))


def _layer_spec(l, shape):
    nd = len(shape)
    return pl.BlockSpec((None,) + tuple(shape), lambda *_: (l,) + (0,) * nd, pipeline_mode=pl.Buffered(1))


def _params(*sem):
    return pltpu.CompilerParams(dimension_semantics=sem, vmem_limit_bytes=VMEM_LIMIT)


def _rope_kernel(pos_ref, invf_ref, sign_ref, c_ref, s_ref):
    ang = pos_ref[...] * invf_ref[...]
    c_ref[...] = jnp.cos(ang)
    s_ref[...] = jnp.sin(ang) * sign_ref[...]


def _rope_tables(positions):
    n = positions.size
    half = RET_DK // 2
    inv_freq = 1.0 / (ROPE_BASE ** (jnp.arange(half, dtype=F32) / half))
    invf2 = jnp.concatenate([inv_freq, inv_freq]).reshape(1, RET_DK)
    sign = jnp.concatenate([-jnp.ones((half,), F32), jnp.ones((half,), F32)]).reshape(1, RET_DK)
    pos = positions.astype(F32).reshape(n, 1)
    t = 1024
    return pl.pallas_call(
        _rope_kernel,
        grid=(n // t,),
        in_specs=[pl.BlockSpec((t, 1), lambda i: (i, 0)),
                  pl.BlockSpec((1, RET_DK), lambda i: (0, 0)),
                  pl.BlockSpec((1, RET_DK), lambda i: (0, 0))],
        out_specs=[pl.BlockSpec((t, RET_DK), lambda i: (i, 0))] * 2,
        out_shape=[jax.ShapeDtypeStruct((n, RET_DK), F32)] * 2,
        compiler_params=_params("arbitrary"),
        name="rope_tables",
    )(pos, invf2, sign)


def _tile_lanes(x, reps):
    while reps > 1:
        x = jnp.concatenate([x, x], axis=1)
        reps //= 2
    return x


def _block_diag(blocks, kt):
    _, r, c = blocks.shape
    x = _tile_lanes(blocks[kt * S5_GPT:(kt + 1) * S5_GPT].reshape(S5_GPT * r, c), S5_GPT)
    row_group = lax.broadcasted_iota(jnp.int32, x.shape, 0) // r
    col_group = lax.broadcasted_iota(jnp.int32, x.shape, 1) // c
    return jnp.where(row_group == col_group, x, 0.0)


def _s5_disc_kernel(lr_ref, li_ref, ls_ref, brt_ref, bit_ref, cr_ref, ci_ref, ar_ref, ai_ref, bp_ref, w_ref):
    lr = lr_ref[0]
    li = li_ref[0]
    step = jnp.exp(ls_ref[0])
    mag = jnp.exp(lr * step)
    a_r = mag * jnp.cos(li * step)
    a_i = mag * jnp.sin(li * step)
    den = lr * lr + li * li
    f_r = ((a_r - 1.0) * lr + a_i * li) / den
    f_i = (a_i * lr - (a_r - 1.0) * li) / den
    br = brt_ref[0]
    bi = bit_ref[0]
    bb_r = f_r * br - f_i * bi
    bb_i = f_r * bi + f_i * br
    cr = cr_ref[0]
    ci = ci_ref[0]
    contract_p = (((2,), (2,)), ((0,), (0,)))
    hi = dict(precision=lax.Precision.HIGHEST, preferred_element_type=F32)
    hc, hs = S5_HG * S5_GROUP_CH, S5_HG * S5_STATE
    pw_r = jnp.ones_like(a_r)
    pw_i = jnp.zeros_like(a_i)
    ro_r, ro_i, lags = [], [], []
    for m in range(S5_R):
        abb_r = pw_r * bb_r - pw_i * bb_i
        abb_i = pw_r * bb_i + pw_i * bb_r
        lag = lax.dot_general(abb_r, cr, contract_p, **hi) - lax.dot_general(abb_i, ci, contract_p, **hi)
        pw_r, pw_i = pw_r * a_r - pw_i * a_i, pw_r * a_i + pw_i * a_r
        cp_r = cr * pw_r - ci * pw_i
        cp_i = cr * pw_i + ci * pw_r
        i = S5_R - 1 - m
        for kt in range(S5_KT):
            bd_r = _block_diag(abb_r, kt)
            bd_i = _block_diag(abb_i, kt)
            for half in range(S5_KT_HALVES):
                rs, cs = slice(half * hc, (half + 1) * hc), slice(half * hs, (half + 1) * hs)
                bp_ref[0, kt, half, i * hc:(i + 1) * hc, :] = jnp.concatenate(
                    [bd_r[rs, cs], bd_i[rs, cs]], axis=1).astype(BF16)
        ro_r.append([_block_diag(cp_r, kt).T for kt in range(S5_KT)])
        ro_i.append([(-_block_diag(cp_i, kt)).T for kt in range(S5_KT)])
        lags.append([_block_diag(lag, kt) for kt in range(S5_KT)])
    zero = jnp.zeros((hc, hc), F32)
    for kt in range(S5_KT):
        for half in range(S5_KT_HALVES):
            rs, cs = slice(half * hs, (half + 1) * hs), slice(half * hc, (half + 1) * hc)
            w_ref[0, kt, half, 0:hs, :] = jnp.concatenate(
                [ro_r[j][kt][rs, cs] for j in range(S5_R)], axis=1).astype(BF16)
            w_ref[0, kt, half, hs:2 * hs, :] = jnp.concatenate(
                [ro_i[j][kt][rs, cs] for j in range(S5_R)], axis=1).astype(BF16)
            for i in range(S5_R):
                w_ref[0, kt, half, 2 * hs + i * hc:2 * hs + (i + 1) * hc, :] = jnp.concatenate(
                    [lags[j - i][kt][cs, cs] if j >= i else zero for j in range(S5_R)], axis=1).astype(BF16)
    ar_ref[0] = pw_r
    ai_ref[0] = pw_i


def _s5_discretise(lam_re, lam_im, log_step, b_re, b_im, c_re, c_im):
    g, p, h = S5_GROUPS, S5_STATE, S5_GROUP_CH
    lr = lam_re.reshape(DEPTH, g, 1, p)
    li = lam_im.reshape(DEPTH, g, 1, p)
    ls = log_step.reshape(DEPTH, g, 1, 1)
    brt = jnp.swapaxes(b_re, -1, -2)
    bit = jnp.swapaxes(b_im, -1, -2)
    vec = pl.BlockSpec((1, g, 1, p), lambda l: (l, 0, 0, 0))
    mat = pl.BlockSpec((1, g, h, p), lambda l: (l, 0, 0, 0))
    return pl.pallas_call(
        _s5_disc_kernel,
        grid=(DEPTH,),
        in_specs=[vec, vec, pl.BlockSpec((1, g, 1, 1), lambda l: (l, 0, 0, 0)), mat, mat, mat, mat],
        out_specs=[vec, vec, pl.BlockSpec((1,) + S5_BP_SHAPE, lambda l: (l, 0, 0, 0, 0)),
                   pl.BlockSpec((1,) + S5_W_SHAPE, lambda l: (l, 0, 0, 0, 0))],
        out_shape=[jax.ShapeDtypeStruct((DEPTH, g, 1, p), F32)] * 2
        + [jax.ShapeDtypeStruct((DEPTH,) + S5_BP_SHAPE, BF16), jax.ShapeDtypeStruct((DEPTH,) + S5_W_SHAPE, BF16)],
        compiler_params=_params("arbitrary"),
        name="s5_discretise",
    )(lr, li, ls, brt, bit, c_re, c_im)


def _memkv_kernel(mem_ref, g_ref, wk_ref, wv_ref, k_ref, v_ref):
    m = _rms(mem_ref[...], g_ref[0]).astype(BF16)
    k_ref[0] = _dot(m, wk_ref[0]).astype(BF16)
    v_ref[0] = _dot(m, wv_ref[0]).astype(BF16)


def _mem_kv(mem2d, norm_mem, w_ck, w_cv):
    rows = mem2d.shape[0]
    wspec = pl.BlockSpec((1, D_MODEL, D_MODEL), lambda l: (l, 0, 0))
    ospec = pl.BlockSpec((1, rows, D_MODEL), lambda l: (l, 0, 0))
    return pl.pallas_call(
        _memkv_kernel,
        grid=(DEPTH,),
        in_specs=[pl.BlockSpec((rows, D_MODEL), lambda l: (0, 0)),
                  pl.BlockSpec((1, 1, D_MODEL), lambda l: (l, 0, 0)), wspec, wspec],
        out_specs=[ospec, ospec],
        out_shape=[jax.ShapeDtypeStruct((DEPTH, rows, D_MODEL), BF16)] * 2,
        compiler_params=_params("arbitrary"),
        name="mem_kv",
    )(mem2d, norm_mem.reshape(DEPTH, 1, D_MODEL), w_ck, w_cv)


def _in_proj_kernel(x_ref, g_ref, w_ref, c_ref, s_ref, u_ref, ub_ref, q_ref, k_ref, v_ref, gate_ref, us_ref):
    h = _rms(x_ref[...], g_ref[...]).astype(BF16)
    cos = c_ref[...]
    sin = s_ref[...]
    u = _dot(h, w_ref[:, 0:S5_WIDTH])
    for c in range(S5_KT):
        us_ref[c] = u[:, c * LANES:(c + 1) * LANES]
    hc = S5_HG * S5_GROUP_CH
    for c in range(S5_KT):
        pieces = [us_ref[c, pl.ds(i, TM // S5_R, stride=S5_R), :] for i in range(S5_R)]
        for i in range(S5_R):
            u_ref[:, i * S5_WIDTH + c * LANES:i * S5_WIDTH + (c + 1) * LANES] = pieces[i]
        for half in range(S5_KT_HALVES):
            cols = jnp.concatenate([p[:, half * hc:(half + 1) * hc] for p in pieces], axis=1)
            base = (c * S5_KT_HALVES + half) * S5_R * hc
            ub_ref[:, base:base + S5_R * hc] = cols.astype(BF16)
    for dst, off, scale in ((q_ref, S5_WIDTH, None), (k_ref, S5_WIDTH + RET_WIDTH, RET_DK ** -0.5)):
        t = _dot(h, w_ref[:, off:off + RET_WIDTH])
        for hd in range(RET_HEADS):
            th = t[:, hd * RET_DK:(hd + 1) * RET_DK]
            r = th * cos + pltpu.roll(th, RET_DK // 2, axis=1) * sin
            if scale is not None:
                r = r * scale
            dst[:, hd * RET_DK:(hd + 1) * RET_DK] = r.astype(BF16)
    off = S5_WIDTH + 2 * RET_WIDTH
    v_ref[...] = _dot(h, w_ref[:, off:off + RET_WIDTH]).astype(BF16)
    gate_ref[...] = _dot(h, w_ref[:, off + RET_WIDTH:off + 2 * RET_WIDTH])


def _blocked_spec(tiles_per_batch):
    return pl.BlockSpec((None, TM // S5_R, S5_R * S5_WIDTH),
                        lambda i: (i // tiles_per_batch, i % tiles_per_batch, 0))


def _in_proj(l, x2d, g, w_bf16, cos2, sin2, nb, seq):
    n = x2d.shape[0]
    row = lambda w: pl.BlockSpec((TM, w), lambda i: (i, 0))
    ret_bf16 = jax.ShapeDtypeStruct((n, RET_WIDTH), BF16)
    return pl.pallas_call(
        _in_proj_kernel,
        grid=(n // TM,),
        in_specs=[row(D_MODEL), _layer_spec(l, (1, D_MODEL)), _layer_spec(l, (D_MODEL, IN_WIDTH)),
                  row(RET_DK), row(RET_DK)],
        out_specs=[_blocked_spec(seq // TM), _blocked_spec(seq // TM)] + [row(RET_WIDTH)] * 4,
        out_shape=[jax.ShapeDtypeStruct((nb, seq // S5_R, S5_R * S5_WIDTH), F32),
                   jax.ShapeDtypeStruct((nb, seq // S5_R, S5_R * S5_WIDTH), BF16),
                   ret_bf16, ret_bf16, ret_bf16,
                   jax.ShapeDtypeStruct((n, RET_WIDTH), F32)],
        scratch_shapes=[pltpu.VMEM((S5_KT, TM, LANES), F32)],
        compiler_params=_params("arbitrary"),
        name="in_proj",
    )(x2d, g, w_bf16, cos2, sin2)


def _s5_kernel(nb, u_ref, ub_ref, bp_ref, ar_ref, ai_ref, w_ref, d_ref, wglu_ref, bglu_ref, gout_ref,
               o_ref, st_ref, y_ref, *xs_refs):
    nslab = S5_SLAB // LANES
    rows = nb * S5_TK

    @pl.when(pl.program_id(0) == 0)
    def _():
        st_ref[...] = jnp.zeros_like(st_ref)

    pack = SUBLANES // nb
    npk = nslab // pack
    hc, hs = S5_HG * S5_GROUP_CH, S5_HG * S5_STATE
    half_slabs = hs // LANES

    def slab_block(s, b):
        return s // pack, ((s % pack) * nb + b) * S5_PITCH

    def half_state_slabs(half):
        re = [half * half_slabs + j for j in range(half_slabs)]
        return re + [nslab + s for s in re]

    def block_inputs(kt, half):
        base = (kt * S5_KT_HALVES + half) * S5_R * hc
        return ub_ref[:, :, base:base + S5_R * hc].reshape(rows, S5_R * hc)

    def state_input(kt, half):
        xb = _dot(block_inputs(kt, half), bp_ref[kt, half])
        for b in range(nb):
            for jj, s in enumerate(half_state_slabs(half)):
                p, r0 = slab_block(s, b)
                xs_refs[kt][p, r0:r0 + S5_TK, :] = xb[b * S5_TK:(b + 1) * S5_TK, jj * LANES:(jj + 1) * LANES]

    def readout(kt):
        def slab(s, b):
            p, r0 = slab_block(s, b)
            return xs_refs[kt][p, r0:r0 + S5_TK, :]

        yk = []
        for half in range(S5_KT_HALVES):
            s_prev = jnp.concatenate(
                [jnp.concatenate([slab(s, b) for s in half_state_slabs(half)], axis=1) for b in range(nb)],
                axis=0).astype(BF16)
            yk.append(_dot(jnp.concatenate([s_prev, block_inputs(kt, half)], axis=1), w_ref[kt, half]))
        for j in range(S5_R):
            at = slice(j * S5_WIDTH + kt * LANES, j * S5_WIDTH + (kt + 1) * LANES)
            y = (jnp.concatenate([h[:, j * hc:(j + 1) * hc] for h in yk], axis=1).reshape(nb, S5_TK, LANES)
                 + d_ref[:, kt * LANES:(kt + 1) * LANES] * u_ref[:, :, at])
            y_ref[:, :, at] = 0.5 * y * (1.0 + lax.erf(y * math.sqrt(0.5)))

    def recurrence(kt):
        def packed(ref, q):
            return jnp.concatenate(
                [jnp.broadcast_to(ref[kt, :, s * LANES:(s + 1) * LANES], (nb, LANES))
                 for s in range(q * pack, (q + 1) * pack)], axis=0)

        a_r = [packed(ar_ref, q) for q in range(npk)]
        a_i = [packed(ai_ref, q) for q in range(npk)]
        s_r = [st_ref[kt, q] for q in range(npk)]
        s_i = [st_ref[kt, npk + q] for q in range(npk)]
        for t in range(S5_TK):
            at = pl.ds(t, pack * nb, stride=S5_PITCH)
            for q in range(npk):
                x_r = xs_refs[kt][q, at, :]
                x_i = xs_refs[kt][npk + q, at, :]
                xs_refs[kt][q, at, :] = s_r[q]
                xs_refs[kt][npk + q, at, :] = s_i[q]
                s_r[q], s_i[q] = (a_r[q] * s_r[q] - a_i[q] * s_i[q] + x_r,
                                  a_r[q] * s_i[q] + a_i[q] * s_r[q] + x_i)
        for q in range(npk):
            st_ref[kt, q] = s_r[q]
            st_ref[kt, npk + q] = s_i[q]

    for kt in range(S5_KT):
        for half in range(S5_KT_HALVES):
            state_input(kt, half)
    for kt in range(S5_KT):
        recurrence(kt)
        readout(kt)

    for i in range(S5_R):
        cols = slice(i * S5_WIDTH, (i + 1) * S5_WIDTH)
        z = y_ref[:, :, cols].reshape(rows, S5_WIDTH)
        z = z * jax.nn.sigmoid(_dot(z.astype(BF16), wglu_ref[...]) + bglu_ref[...])
        o_ref[:, :, cols] = _rms(z, gout_ref[...]).astype(BF16).reshape(nb, S5_TK, S5_WIDTH)


def _s5_mixer(l, u3, ub3, bp, a_r, a_i, w, d, wglu, bglu, gout):
    nb, krows, width = u3.shape
    assert SUBLANES % nb == 0
    nbuf = 2 * S5_SLAB // LANES * nb // SUBLANES
    tok = pl.BlockSpec((nb, S5_TK, width), lambda c: (0, c, 0))
    return pl.pallas_call(
        functools.partial(_s5_kernel, nb),
        grid=(krows // S5_TK,),
        in_specs=[tok, tok,
                  _layer_spec(l, S5_BP_SHAPE),
                  _layer_spec(l, (S5_KT, 1, S5_SLAB)), _layer_spec(l, (S5_KT, 1, S5_SLAB)),
                  _layer_spec(l, S5_W_SHAPE),
                  _layer_spec(l, (1, S5_WIDTH)), _layer_spec(l, (S5_WIDTH, S5_WIDTH)),
                  _layer_spec(l, (1, S5_WIDTH)), _layer_spec(l, (1, S5_WIDTH))],
        out_specs=tok,
        out_shape=jax.ShapeDtypeStruct((nb, krows, width), BF16),
        scratch_shapes=[pltpu.VMEM((S5_KT, nbuf, SUBLANES, LANES), F32),
                        pltpu.VMEM((nb, S5_TK, width), F32)]
        + [pltpu.VMEM((nbuf, SUBLANES * S5_PITCH, LANES), F32)] * S5_KT,
        compiler_params=_params("arbitrary"),
        name="s5_mixer",
    )(u3, ub3, bp, a_r, a_i, w, d, wglu, bglu, gout)


def _ret_kernel(q_ref, k_ref, v_ref, g_ref, lg_ref, gain_ref, o_ref, st_ref, dm_ref, zeta_ref, xi_ref):
    @pl.when((pl.program_id(0) == 0) & (pl.program_id(1) == 0))
    def _():
        row = lax.broadcasted_iota(jnp.int32, (RET_CHUNK, RET_CHUNK), 0).astype(F32)
        col = lax.broadcasted_iota(jnp.int32, (RET_CHUNK, RET_CHUNK), 1).astype(F32)
        diff = row - col
        idx = lax.broadcasted_iota(jnp.int32, (RET_CHUNK, RET_DK), 0).astype(F32)
        for hd in range(RET_HEADS):
            lg = lg_ref[hd]
            lg_wide = jnp.concatenate([lg] * (RET_CHUNK // RET_DK), axis=1)
            dm_ref[hd] = jnp.where(diff >= 0, jnp.exp(jnp.maximum(diff, 0.0) * lg_wide), 0.0)
            zeta_ref[hd] = jnp.exp((RET_CHUNK - 1.0 - idx) * lg)
            xi_ref[hd] = jnp.exp((idx + 1.0) * lg)

    @pl.when(pl.program_id(1) == 0)
    def _():
        st_ref[...] = jnp.zeros_like(st_ref)

    for c in range(RET_TR // RET_CHUNK):
        rows = slice(c * RET_CHUNK, (c + 1) * RET_CHUNK)
        for hd in range(RET_HEADS):
            lanes = slice(hd * RET_DK, (hd + 1) * RET_DK)
            qh = q_ref[rows, lanes]
            kh = k_ref[rows, lanes]
            vh = v_ref[rows, lanes]
            s = lax.dot_general(qh, kh, (((1,), (1,)), ((), ())), preferred_element_type=F32)
            inner = _dot((s * dm_ref[hd]).astype(BF16), vh)
            state = st_ref[hd]
            cross = _dot(qh, state.astype(BF16)) * xi_ref[hd]
            kz = (kh.astype(F32) * zeta_ref[hd]).astype(BF16)
            kv = lax.dot_general(kz, vh, (((0,), (0,)), ((), ())), preferred_element_type=F32)
            st_ref[hd] = jnp.exp(RET_CHUNK * lg_ref[hd]) * state + kv
            o = inner + cross
            mu = jnp.mean(o, axis=-1, keepdims=True)
            oc = o - mu
            var = jnp.mean(oc * oc, axis=-1, keepdims=True)
            o = oc * lax.rsqrt(var + EPS) * gain_ref[hd]
            o_ref[rows, lanes] = (o * jax.nn.silu(g_ref[rows, lanes])).astype(BF16)


def _retention(l, q, k, v, gate, lg, gain, nb, seq):
    steps = seq // RET_TR
    tok = pl.BlockSpec((RET_TR, RET_WIDTH), lambda b, c: (b * steps + c, 0))
    return pl.pallas_call(
        _ret_kernel,
        grid=(nb, steps),
        in_specs=[tok, tok, tok, tok, _const_spec((RET_HEADS, 1, RET_DK)),
                  _layer_spec(l, (RET_HEADS, 1, RET_DK))],
        out_specs=tok,
        out_shape=jax.ShapeDtypeStruct((nb * seq, RET_WIDTH), BF16),
        scratch_shapes=[pltpu.VMEM((RET_HEADS, RET_DK, RET_DK), F32),
                        pltpu.VMEM((RET_HEADS, RET_CHUNK, RET_CHUNK), F32),
                        pltpu.VMEM((RET_HEADS, RET_CHUNK, RET_DK), F32),
                        pltpu.VMEM((RET_HEADS, RET_CHUNK, RET_DK), F32)],
        compiler_params=_params("arbitrary", "arbitrary"),
        name="retention",
    )(q, k, v, gate, lg, gain)


def _mix_out_kernel(x_ref, ys_ref, yr_ref, wo_ref, g_ref, wq_ref, k_ref, v_ref, wco_ref, o_ref, ps_ref):
    tk = TM // S5_R
    ys = jnp.concatenate([ys_ref[:, i * S5_WIDTH:(i + 1) * S5_WIDTH] for i in range(S5_R)], axis=0)
    ps = _dot(ys, wo_ref[0:S5_WIDTH, :])
    for i in range(S5_R):
        for c in range(D_MODEL // LANES):
            ps_ref[c, pl.ds(i, tk, stride=S5_R), :] = ps[i * tk:(i + 1) * tk, c * LANES:(c + 1) * LANES]
    ssm = jnp.concatenate([ps_ref[c] for c in range(D_MODEL // LANES)], axis=1)
    x = x_ref[...] + ssm + _dot(yr_ref[...], wo_ref[S5_WIDTH:, :])
    h = _rms(x, g_ref[...]).astype(BF16)
    q = _dot(h, wq_ref[...]).astype(BF16)
    outs = []
    for hd in range(X_HEADS):
        lanes = slice(hd * X_HEAD_DIM, (hd + 1) * X_HEAD_DIM)
        s = lax.dot_general(q[:, lanes], k_ref[:, lanes], (((1,), (1,)), ((), ())),
                            preferred_element_type=F32)
        e = jnp.exp(s - jnp.max(s, axis=-1, keepdims=True))
        p = e * (1.0 / jnp.sum(e, axis=-1, keepdims=True))
        outs.append(_dot(p.astype(BF16), v_ref[:, lanes]).astype(BF16))
    o = jnp.concatenate(outs, axis=1)
    o_ref[...] = x + _dot(o, wco_ref[...])


def _mix_out(l, x2d, y_ssm, y_ret, w_out, g, wq, k_mem, v_mem, wco, tiles_per_batch):
    n = x2d.shape[0]
    row = lambda w: pl.BlockSpec((TM, w), lambda i: (i, 0))
    mem = pl.BlockSpec((None, MEM_LEN, D_MODEL), lambda i: (l, i // tiles_per_batch, 0))
    sq = _layer_spec(l, (D_MODEL, D_MODEL))
    return pl.pallas_call(
        _mix_out_kernel,
        grid=(n // TM,),
        in_specs=[row(D_MODEL), _blocked_spec(tiles_per_batch),
                  row(RET_WIDTH), sq, _layer_spec(l, (1, D_MODEL)), sq, mem, mem, sq],
        out_specs=row(D_MODEL),
        out_shape=jax.ShapeDtypeStruct((n, D_MODEL), F32),
        scratch_shapes=[pltpu.VMEM((D_MODEL // LANES, TM, LANES), F32)],
        compiler_params=_params("arbitrary"),
        name="mix_out_xattn",
    )(x2d, y_ssm, y_ret, w_out, g, wq, k_mem, v_mem, wco)


def _ffn_kernel(final, x_ref, g_ref, wg_ref, wu_ref, wd_ref, gf_ref, o_ref):
    x = x_ref[...]
    h = _rms(x, g_ref[...]).astype(BF16)
    acts = [(jax.nn.silu(_dot(h, wg_ref[:, lo:hi])) * _dot(h, wu_ref[:, lo:hi])).astype(BF16)
            for lo, hi in FF_SPLITS]
    acc = x + _dot(jnp.concatenate(acts, axis=1), wd_ref[...])
    if final:
        acc = _rms(acc, gf_ref[...])
    o_ref[...] = acc


def _ffn(l, x2d, g, wg, wu, wd, g_final, final):
    n = x2d.shape[0]
    row = pl.BlockSpec((TM, D_MODEL), lambda i: (i, 0))
    return pl.pallas_call(
        functools.partial(_ffn_kernel, final),
        grid=(n // TM,),
        in_specs=[row, _layer_spec(l, (1, D_MODEL)), _layer_spec(l, (D_MODEL, D_FF)),
                  _layer_spec(l, (D_MODEL, D_FF)), _layer_spec(l, (D_FF, D_MODEL)), _const_spec((1, D_MODEL))],
        out_specs=row,
        out_shape=jax.ShapeDtypeStruct((n, D_MODEL), F32),
        compiler_params=_params("arbitrary"),
        name="ffn_final" if final else "ffn",
    )(x2d, g, wg, wu, wd, g_final.reshape(1, D_MODEL))


def kernel(x, mem, positions, norm_mix, w_in, s5_lambda_re, s5_lambda_im, s5_log_step, s5_b_re, s5_b_im, s5_c_re, s5_c_im, s5_d, s5_w_glu, s5_b_glu, s5_out_norm, ret_out_norm, w_out, norm_cross, norm_mem, w_cq, w_ck, w_cv, w_co, norm_ffn, w_gate, w_up, w_down, norm_final):
    nb, seq, _ = x.shape
    n = nb * seq
    assert seq % RET_TR == 0 and seq % (S5_R * S5_TK) == 0 and seq % TM == 0
    bf = lambda a: a.astype(BF16)
    vec = lambda a: a.reshape(DEPTH, 1, a.shape[-1])

    cos2, sin2 = _rope_tables(positions)
    a_r, a_i, s5_bp, s5_w = _s5_discretise(
        s5_lambda_re, s5_lambda_im, s5_log_step, s5_b_re, s5_b_im, s5_c_re, s5_c_im)
    a_r = a_r.reshape(DEPTH, S5_KT, 1, S5_SLAB)
    a_i = a_i.reshape(DEPTH, S5_KT, 1, S5_SLAB)
    k_mem, v_mem = _mem_kv(mem.reshape(nb * MEM_LEN, D_MODEL), norm_mem, bf(w_ck), bf(w_cv))
    lg = jnp.log1p(-jnp.exp2(-5.0 - jnp.arange(RET_HEADS, dtype=F32)))
    lg = jnp.broadcast_to(lg[:, None, None], (RET_HEADS, 1, RET_DK))
    ret_gain = ret_out_norm.reshape(DEPTH, RET_HEADS, 1, RET_DK)
    w_in_b, w_out_b, w_cq_b, w_co_b = bf(w_in), bf(w_out), bf(w_cq * X_HEAD_DIM ** -0.5), bf(w_co)
    w_gate_b, w_up_b, w_down_b, w_glu_b = bf(w_gate), bf(w_up), bf(w_down), bf(s5_w_glu)
    g_mix, g_cross, g_ffn = vec(norm_mix), vec(norm_cross), vec(norm_ffn)
    s5_d3, s5_bglu3, s5_gout3 = vec(s5_d), vec(s5_b_glu), vec(s5_out_norm)

    x2d = x.reshape(n, D_MODEL)
    for l in range(DEPTH):
        u, ub, q, k, v, gate = _in_proj(l, x2d, g_mix, w_in_b, cos2, sin2, nb, seq)
        y_ssm = _s5_mixer(l, u, ub, s5_bp, a_r, a_i, s5_w,
                          s5_d3, w_glu_b, s5_bglu3, s5_gout3)
        y_ret = _retention(l, q, k, v, gate, lg, ret_gain, nb, seq)
        x2d = _mix_out(l, x2d, y_ssm, y_ret, w_out_b, g_cross, w_cq_b,
                       k_mem, v_mem, w_co_b, seq // TM)
        x2d = _ffn(l, x2d, g_ffn, w_gate_b, w_up_b, w_down_b, norm_final, l == DEPTH - 1)
    return x2d.reshape(nb, seq, D_MODEL)
```

```python
import functools
import math

import jax
import jax.numpy as jnp
from jax import lax
from jax.experimental import pallas as pl
from jax.experimental.pallas import tpu as pltpu

D_MODEL = 1024
DEPTH = 4
MEM_LEN = 256
S5_WIDTH = 512
S5_GROUP_CH = 16
S5_GROUPS = 32
S5_STATE = 64
RET_HEADS = 4
RET_DK = 128
RET_WIDTH = 512
IN_WIDTH = S5_WIDTH + 4 * RET_WIDTH
X_HEADS = 4
X_HEAD_DIM = 256
D_FF = 2816
ROPE_BASE = 10000.0
EPS = 1e-6

LANES = 128
SUBLANES = 8
S5_KT = S5_WIDTH // LANES
S5_GPT = LANES // S5_GROUP_CH
S5_SLAB = S5_GPT * S5_STATE
MXU_DIM = 256
S5_HG = MXU_DIM // S5_STATE
S5_KT_HALVES = S5_GPT // S5_HG
S5_R = 4
S5_BP_SHAPE = (S5_KT, S5_KT_HALVES, S5_R * S5_HG * S5_GROUP_CH, 2 * S5_HG * S5_STATE)
S5_W_SHAPE = (S5_KT, S5_KT_HALVES, (2 * S5_STATE + S5_R * S5_GROUP_CH) * S5_HG,
              S5_R * S5_HG * S5_GROUP_CH)
S5_TK = 128
S5_PITCH = S5_TK + 8
RET_CHUNK = 256
RET_TR = 2048
TM = 1024
FF_SPLITS = ((0, 1536), (1536, D_FF))

BF16 = jnp.bfloat16
F32 = jnp.float32
VMEM_LIMIT = 48 * 1024 * 1024


def _dot(a, b):
    return jnp.dot(a, b, preferred_element_type=F32)


def _rms(x, g):
    ms = jnp.mean(x * x, axis=-1, keepdims=True)
    return x * lax.rsqrt(ms + EPS) * g


def _const_spec(shape):
    nd = len(shape)
    return pl.BlockSpec(shape, lambda *_: (0,) * nd, pipeline_mode=pl.Buffered(1))


def _layer_spec(l, shape):
    nd = len(shape)
    return pl.BlockSpec((None,) + tuple(shape), lambda *_: (l,) + (0,) * nd, pipeline_mode=pl.Buffered(1))


def _params(*sem):
    return pltpu.CompilerParams(dimension_semantics=sem, vmem_limit_bytes=VMEM_LIMIT)


def _rope_kernel(pos_ref, invf_ref, c_ref, s_ref):
    low = lax.broadcasted_iota(jnp.int32, c_ref.shape, 1) < RET_DK // 2
    ang = jnp.where(low, pos_ref[:, 0:1], pos_ref[:, 1:2]) * invf_ref[...]
    c_ref[...] = jnp.cos(ang)
    s_ref[...] = jnp.sin(ang)


def _rope_tables(positions):
    n = positions.size
    half = RET_DK // 2
    inv_freq = 1.0 / (ROPE_BASE ** (jnp.arange(half, dtype=F32) / half))
    invf2 = jnp.concatenate([inv_freq, inv_freq]).reshape(1, RET_DK)
    pos = positions.astype(F32).reshape(n // TM, 2, TM // 2)
    pos = jnp.swapaxes(pos, 1, 2).reshape(n // 2, 2)
    t = 1024
    return pl.pallas_call(
        _rope_kernel,
        grid=(n // 2 // t,),
        in_specs=[pl.BlockSpec((t, 2), lambda i: (i, 0)),
                  pl.BlockSpec((1, RET_DK), lambda i: (0, 0))],
        out_specs=[pl.BlockSpec((t, RET_DK), lambda i: (i, 0))] * 2,
        out_shape=[jax.ShapeDtypeStruct((n // 2, RET_DK), F32)] * 2,
        compiler_params=_params("arbitrary"),
        name="rope_tables",
    )(pos, invf2)


def _unpack_rope(c_ref, s_ref):
    half = RET_DK // 2
    low = lax.broadcasted_iota(jnp.int32, c_ref.shape, 1) < half
    c, s = c_ref[...], s_ref[...]
    c_sw, s_sw = pltpu.roll(c, half, axis=1), pltpu.roll(s, half, axis=1)
    cos = jnp.concatenate([jnp.where(low, c, c_sw), jnp.where(low, c_sw, c)], axis=0)
    sin = jnp.concatenate([jnp.where(low, -s, s_sw), jnp.where(low, -s_sw, s)], axis=0)
    return cos, sin


def _tile_lanes(x, reps):
    while reps > 1:
        x = jnp.concatenate([x, x], axis=1)
        reps //= 2
    return x


def _block_diag(blocks, kt):
    _, r, c = blocks.shape
    x = _tile_lanes(blocks[kt * S5_GPT:(kt + 1) * S5_GPT].reshape(S5_GPT * r, c), S5_GPT)
    row_group = lax.broadcasted_iota(jnp.int32, x.shape, 0) // r
    col_group = lax.broadcasted_iota(jnp.int32, x.shape, 1) // c
    return jnp.where(row_group == col_group, x, 0.0)


def _s5_disc_kernel(lr_ref, li_ref, ls_ref, brt_ref, bit_ref, cr_ref, ci_ref, ar_ref, ai_ref, bp_ref, w_ref):
    lr = lr_ref[0]
    li = li_ref[0]
    step = jnp.exp(ls_ref[0])
    mag = jnp.exp(lr * step)
    a_r = mag * jnp.cos(li * step)
    a_i = mag * jnp.sin(li * step)
    den = lr * lr + li * li
    f_r = ((a_r - 1.0) * lr + a_i * li) / den
    f_i = (a_i * lr - (a_r - 1.0) * li) / den
    br = brt_ref[0]
    bi = bit_ref[0]
    bb_r = f_r * br - f_i * bi
    bb_i = f_r * bi + f_i * br
    cr = cr_ref[0]
    ci = ci_ref[0]
    contract_p = (((2,), (2,)), ((0,), (0,)))
    hi = dict(precision=lax.Precision.HIGHEST, preferred_element_type=F32)
    hc, hs = S5_HG * S5_GROUP_CH, S5_HG * S5_STATE
    pw_r = jnp.ones_like(a_r)
    pw_i = jnp.zeros_like(a_i)
    ro_r, ro_i, lags = [], [], []
    for m in range(S5_R):
        abb_r = pw_r * bb_r - pw_i * bb_i
        abb_i = pw_r * bb_i + pw_i * bb_r
        lag = lax.dot_general(abb_r, cr, contract_p, **hi) - lax.dot_general(abb_i, ci, contract_p, **hi)
        pw_r, pw_i = pw_r * a_r - pw_i * a_i, pw_r * a_i + pw_i * a_r
        cp_r = cr * pw_r - ci * pw_i
        cp_i = cr * pw_i + ci * pw_r
        i = S5_R - 1 - m
        for kt in range(S5_KT):
            bd_r = _block_diag(abb_r, kt)
            bd_i = _block_diag(abb_i, kt)
            for half in range(S5_KT_HALVES):
                rs, cs = slice(half * hc, (half + 1) * hc), slice(half * hs, (half + 1) * hs)
                bp_ref[0, kt, half, i * hc:(i + 1) * hc, :] = jnp.concatenate(
                    [bd_r[rs, cs], bd_i[rs, cs]], axis=1).astype(BF16)
        ro_r.append([_block_diag(cp_r, kt).T for kt in range(S5_KT)])
        ro_i.append([(-_block_diag(cp_i, kt)).T for kt in range(S5_KT)])
        lags.append([_block_diag(lag, kt) for kt in range(S5_KT)])
    zero = jnp.zeros((hc, hc), F32)
    for kt in range(S5_KT):
        for half in range(S5_KT_HALVES):
            rs, cs = slice(half * hs, (half + 1) * hs), slice(half * hc, (half + 1) * hc)
            w_ref[0, kt, half, 0:hs, :] = jnp.concatenate(
                [ro_r[j][kt][rs, cs] for j in range(S5_R)], axis=1).astype(BF16)
            w_ref[0, kt, half, hs:2 * hs, :] = jnp.concatenate(
                [ro_i[j][kt][rs, cs] for j in range(S5_R)], axis=1).astype(BF16)
            for i in range(S5_R):
                w_ref[0, kt, half, 2 * hs + i * hc:2 * hs + (i + 1) * hc, :] = jnp.concatenate(
                    [lags[j - i][kt][cs, cs] if j >= i else zero for j in range(S5_R)], axis=1).astype(BF16)
    ar_ref[0] = pw_r
    ai_ref[0] = pw_i


def _s5_discretise(lam_re, lam_im, log_step, b_re, b_im, c_re, c_im):
    g, p, h = S5_GROUPS, S5_STATE, S5_GROUP_CH
    lr = lam_re.reshape(DEPTH, g, 1, p)
    li = lam_im.reshape(DEPTH, g, 1, p)
    ls = log_step.reshape(DEPTH, g, 1, 1)
    brt = jnp.swapaxes(b_re, -1, -2)
    bit = jnp.swapaxes(b_im, -1, -2)
    vec = pl.BlockSpec((1, g, 1, p), lambda l: (l, 0, 0, 0))
    mat = pl.BlockSpec((1, g, h, p), lambda l: (l, 0, 0, 0))
    return pl.pallas_call(
        _s5_disc_kernel,
        grid=(DEPTH,),
        in_specs=[vec, vec, pl.BlockSpec((1, g, 1, 1), lambda l: (l, 0, 0, 0)), mat, mat, mat, mat],
        out_specs=[vec, vec, pl.BlockSpec((1,) + S5_BP_SHAPE, lambda l: (l, 0, 0, 0, 0)),
                   pl.BlockSpec((1,) + S5_W_SHAPE, lambda l: (l, 0, 0, 0, 0))],
        out_shape=[jax.ShapeDtypeStruct((DEPTH, g, 1, p), F32)] * 2
        + [jax.ShapeDtypeStruct((DEPTH,) + S5_BP_SHAPE, BF16), jax.ShapeDtypeStruct((DEPTH,) + S5_W_SHAPE, BF16)],
        compiler_params=_params("arbitrary"),
        name="s5_discretise",
    )(lr, li, ls, brt, bit, c_re, c_im)


def _memkv_kernel(mem_ref, g_ref, wk_ref, wv_ref, k_ref, v_ref):
    m = _rms(mem_ref[...], g_ref[0]).astype(BF16)
    k_ref[0] = _dot(m, wk_ref[0]).astype(BF16)
    v_ref[0] = _dot(m, wv_ref[0]).astype(BF16)


def _mem_kv(mem2d, norm_mem, w_ck, w_cv):
    rows = mem2d.shape[0]
    wspec = pl.BlockSpec((1, D_MODEL, D_MODEL), lambda l: (l, 0, 0))
    ospec = pl.BlockSpec((1, rows, D_MODEL), lambda l: (l, 0, 0))
    return pl.pallas_call(
        _memkv_kernel,
        grid=(DEPTH,),
        in_specs=[pl.BlockSpec((rows, D_MODEL), lambda l: (0, 0)),
                  pl.BlockSpec((1, 1, D_MODEL), lambda l: (l, 0, 0)), wspec, wspec],
        out_specs=[ospec, ospec],
        out_shape=[jax.ShapeDtypeStruct((DEPTH, rows, D_MODEL), BF16)] * 2,
        compiler_params=_params("arbitrary"),
        name="mem_kv",
    )(mem2d, norm_mem.reshape(DEPTH, 1, D_MODEL), w_ck, w_cv)


def _in_proj_kernel(x_ref, g_ref, w_ref, c_ref, s_ref, u_ref, ub_ref, q_ref, k_ref, v_ref, gate_ref, us_ref):
    h = _rms(x_ref[...], g_ref[...]).astype(BF16)
    cos, sin = _unpack_rope(c_ref, s_ref)
    u = _dot(h, w_ref[:, 0:S5_WIDTH])
    for c in range(S5_KT):
        us_ref[c] = u[:, c * LANES:(c + 1) * LANES]
    hc = S5_HG * S5_GROUP_CH
    for c in range(S5_KT):
        pieces = [us_ref[c, pl.ds(i, TM // S5_R, stride=S5_R), :] for i in range(S5_R)]
        for i in range(S5_R):
            u_ref[:, i * S5_WIDTH + c * LANES:i * S5_WIDTH + (c + 1) * LANES] = pieces[i]
        for half in range(S5_KT_HALVES):
            cols = jnp.concatenate([p[:, half * hc:(half + 1) * hc] for p in pieces], axis=1)
            base = (c * S5_KT_HALVES + half) * S5_R * hc
            ub_ref[:, base:base + S5_R * hc] = cols.astype(BF16)
    for dst, off, scale in ((q_ref, S5_WIDTH, None), (k_ref, S5_WIDTH + RET_WIDTH, RET_DK ** -0.5)):
        t = _dot(h, w_ref[:, off:off + RET_WIDTH])
        for hd in range(RET_HEADS):
            th = t[:, hd * RET_DK:(hd + 1) * RET_DK]
            r = th * cos + pltpu.roll(th, RET_DK // 2, axis=1) * sin
            if scale is not None:
                r = r * scale
            dst[:, hd * RET_DK:(hd + 1) * RET_DK] = r.astype(BF16)
    off = S5_WIDTH + 2 * RET_WIDTH
    v_ref[...] = _dot(h, w_ref[:, off:off + RET_WIDTH]).astype(BF16)
    gate_ref[...] = _dot(h, w_ref[:, off + RET_WIDTH:off + 2 * RET_WIDTH])


def _blocked_spec(tiles_per_batch):
    return pl.BlockSpec((None, TM // S5_R, S5_R * S5_WIDTH),
                        lambda i: (i // tiles_per_batch, i % tiles_per_batch, 0))


def _in_proj(l, x2d, g, w_bf16, cos2, sin2, nb, seq):
    n = x2d.shape[0]
    row = lambda w: pl.BlockSpec((TM, w), lambda i: (i, 0))
    ret_bf16 = jax.ShapeDtypeStruct((n, RET_WIDTH), BF16)
    return pl.pallas_call(
        _in_proj_kernel,
        grid=(n // TM,),
        in_specs=[row(D_MODEL), _layer_spec(l, (1, D_MODEL)), _layer_spec(l, (D_MODEL, IN_WIDTH)),
                  pl.BlockSpec((TM // 2, RET_DK), lambda i: (i, 0)), pl.BlockSpec((TM // 2, RET_DK), lambda i: (i, 0))],
        out_specs=[_blocked_spec(seq // TM), _blocked_spec(seq // TM)] + [row(RET_WIDTH)] * 4,
        out_shape=[jax.ShapeDtypeStruct((nb, seq // S5_R, S5_R * S5_WIDTH), F32),
                   jax.ShapeDtypeStruct((nb, seq // S5_R, S5_R * S5_WIDTH), BF16),
                   ret_bf16, ret_bf16, ret_bf16,
                   jax.ShapeDtypeStruct((n, RET_WIDTH), F32)],
        scratch_shapes=[pltpu.VMEM((S5_KT, TM, LANES), F32)],
        compiler_params=_params("arbitrary"),
        name="in_proj",
    )(x2d, g, w_bf16, cos2, sin2)


def _s5_kernel(nb, u_ref, ub_ref, bp_ref, ar_ref, ai_ref, w_ref, d_ref, wglu_ref, bglu_ref, gout_ref,
               o_ref, st_ref, y_ref, *xs_refs):
    nslab = S5_SLAB // LANES
    rows = nb * S5_TK

    @pl.when(pl.program_id(0) == 0)
    def _():
        st_ref[...] = jnp.zeros_like(st_ref)

    pack = SUBLANES // nb
    npk = nslab // pack
    hc, hs = S5_HG * S5_GROUP_CH, S5_HG * S5_STATE
    half_slabs = hs // LANES

    def slab_block(s, b):
        return s // pack, ((s % pack) * nb + b) * S5_PITCH

    def half_state_slabs(half):
        re = [half * half_slabs + j for j in range(half_slabs)]
        return re + [nslab + s for s in re]

    def block_inputs(kt, half):
        base = (kt * S5_KT_HALVES + half) * S5_R * hc
        return ub_ref[:, :, base:base + S5_R * hc].reshape(rows, S5_R * hc)

    def state_input(kt, half):
        xb = _dot(block_inputs(kt, half), bp_ref[kt, half])
        for b in range(nb):
            for jj, s in enumerate(half_state_slabs(half)):
                p, r0 = slab_block(s, b)
                xs_refs[kt][p, r0:r0 + S5_TK, :] = xb[b * S5_TK:(b + 1) * S5_TK, jj * LANES:(jj + 1) * LANES]

    def readout(kt):
        def slab(s, b):
            p, r0 = slab_block(s, b)
            return xs_refs[kt][p, r0:r0 + S5_TK, :]

        yk = []
        for half in range(S5_KT_HALVES):
            s_prev = jnp.concatenate(
                [jnp.concatenate([slab(s, b) for s in half_state_slabs(half)], axis=1) for b in range(nb)],
                axis=0).astype(BF16)
            yk.append(_dot(jnp.concatenate([s_prev, block_inputs(kt, half)], axis=1), w_ref[kt, half]))
        for j in range(S5_R):
            at = slice(j * S5_WIDTH + kt * LANES, j * S5_WIDTH + (kt + 1) * LANES)
            y = (jnp.concatenate([h[:, j * hc:(j + 1) * hc] for h in yk], axis=1).reshape(nb, S5_TK, LANES)
                 + d_ref[:, kt * LANES:(kt + 1) * LANES] * u_ref[:, :, at])
            y_ref[:, :, at] = 0.5 * y * (1.0 + lax.erf(y * math.sqrt(0.5)))

    def recurrence(kt):
        def packed(ref, q):
            return jnp.concatenate(
                [jnp.broadcast_to(ref[kt, :, s * LANES:(s + 1) * LANES], (nb, LANES))
                 for s in range(q * pack, (q + 1) * pack)], axis=0)

        a_r = [packed(ar_ref, q) for q in range(npk)]
        a_i = [packed(ai_ref, q) for q in range(npk)]
        s_r = [st_ref[kt, q] for q in range(npk)]
        s_i = [st_ref[kt, npk + q] for q in range(npk)]
        for t in range(S5_TK):
            at = pl.ds(t, pack * nb, stride=S5_PITCH)
            for q in range(npk):
                x_r = xs_refs[kt][q, at, :]
                x_i = xs_refs[kt][npk + q, at, :]
                xs_refs[kt][q, at, :] = s_r[q]
                xs_refs[kt][npk + q, at, :] = s_i[q]
                s_r[q], s_i[q] = (a_r[q] * s_r[q] - a_i[q] * s_i[q] + x_r,
                                  a_r[q] * s_i[q] + a_i[q] * s_r[q] + x_i)
        for q in range(npk):
            st_ref[kt, q] = s_r[q]
            st_ref[kt, npk + q] = s_i[q]

    for kt in range(S5_KT):
        for half in range(S5_KT_HALVES):
            state_input(kt, half)
    for kt in range(S5_KT):
        recurrence(kt)
        readout(kt)

    for i in range(S5_R):
        cols = slice(i * S5_WIDTH, (i + 1) * S5_WIDTH)
        z = y_ref[:, :, cols].reshape(rows, S5_WIDTH)
        z = z * jax.nn.sigmoid(_dot(z.astype(BF16), wglu_ref[...]) + bglu_ref[...])
        o_ref[:, :, cols] = _rms(z, gout_ref[...]).astype(BF16).reshape(nb, S5_TK, S5_WIDTH)


def _s5_mixer(l, u3, ub3, bp, a_r, a_i, w, d, wglu, bglu, gout):
    nb, krows, width = u3.shape
    assert SUBLANES % nb == 0
    nbuf = 2 * S5_SLAB // LANES * nb // SUBLANES
    tok = pl.BlockSpec((nb, S5_TK, width), lambda c: (0, c, 0))
    return pl.pallas_call(
        functools.partial(_s5_kernel, nb),
        grid=(krows // S5_TK,),
        in_specs=[tok, tok,
                  _layer_spec(l, S5_BP_SHAPE),
                  _layer_spec(l, (S5_KT, 1, S5_SLAB)), _layer_spec(l, (S5_KT, 1, S5_SLAB)),
                  _layer_spec(l, S5_W_SHAPE),
                  _layer_spec(l, (1, S5_WIDTH)), _layer_spec(l, (S5_WIDTH, S5_WIDTH)),
                  _layer_spec(l, (1, S5_WIDTH)), _layer_spec(l, (1, S5_WIDTH))],
        out_specs=tok,
        out_shape=jax.ShapeDtypeStruct((nb, krows, width), BF16),
        scratch_shapes=[pltpu.VMEM((S5_KT, nbuf, SUBLANES, LANES), F32),
                        pltpu.VMEM((nb, S5_TK, width), F32)]
        + [pltpu.VMEM((nbuf, SUBLANES * S5_PITCH, LANES), F32)] * S5_KT,
        compiler_params=_params("arbitrary"),
        name="s5_mixer",
    )(u3, ub3, bp, a_r, a_i, w, d, wglu, bglu, gout)


def _ret_kernel(q_ref, k_ref, v_ref, g_ref, lg_ref, gain_ref, o_ref, st_ref, dm_ref, zeta_ref, xi_ref):
    @pl.when((pl.program_id(0) == 0) & (pl.program_id(1) == 0))
    def _():
        row = lax.broadcasted_iota(jnp.int32, (RET_CHUNK, RET_CHUNK), 0).astype(F32)
        col = lax.broadcasted_iota(jnp.int32, (RET_CHUNK, RET_CHUNK), 1).astype(F32)
        diff = row - col
        idx = lax.broadcasted_iota(jnp.int32, (RET_CHUNK, RET_DK), 0).astype(F32)
        for hd in range(RET_HEADS):
            lg = lg_ref[hd]
            lg_wide = jnp.concatenate([lg] * (RET_CHUNK // RET_DK), axis=1)
            dm_ref[hd] = jnp.where(diff >= 0, jnp.exp(jnp.maximum(diff, 0.0) * lg_wide), 0.0)
            zeta_ref[hd] = jnp.exp((RET_CHUNK - 1.0 - idx) * lg)
            xi_ref[hd] = jnp.exp((idx + 1.0) * lg)

    @pl.when(pl.program_id(1) == 0)
    def _():
        st_ref[...] = jnp.zeros_like(st_ref)

    for c in range(RET_TR // RET_CHUNK):
        rows = slice(c * RET_CHUNK, (c + 1) * RET_CHUNK)
        for hd in range(RET_HEADS):
            lanes = slice(hd * RET_DK, (hd + 1) * RET_DK)
            qh = q_ref[rows, lanes]
            kh = k_ref[rows, lanes]
            vh = v_ref[rows, lanes]
            s = lax.dot_general(qh, kh, (((1,), (1,)), ((), ())), preferred_element_type=F32)
            inner = _dot((s * dm_ref[hd]).astype(BF16), vh)
            state = st_ref[hd]
            cross = _dot(qh, state.astype(BF16)) * xi_ref[hd]
            kz = (kh.astype(F32) * zeta_ref[hd]).astype(BF16)
            kv = lax.dot_general(kz, vh, (((0,), (0,)), ((), ())), preferred_element_type=F32)
            st_ref[hd] = jnp.exp(RET_CHUNK * lg_ref[hd]) * state + kv
            o = inner + cross
            mu = jnp.mean(o, axis=-1, keepdims=True)
            oc = o - mu
            var = jnp.mean(oc * oc, axis=-1, keepdims=True)
            o = oc * lax.rsqrt(var + EPS) * gain_ref[hd]
            o_ref[rows, lanes] = (o * jax.nn.silu(g_ref[rows, lanes])).astype(BF16)


def _retention(l, q, k, v, gate, lg, gain, nb, seq):
    steps = seq // RET_TR
    tok = pl.BlockSpec((RET_TR, RET_WIDTH), lambda b, c: (b * steps + c, 0))
    return pl.pallas_call(
        _ret_kernel,
        grid=(nb, steps),
        in_specs=[tok, tok, tok, tok, _const_spec((RET_HEADS, 1, RET_DK)),
                  _layer_spec(l, (RET_HEADS, 1, RET_DK))],
        out_specs=tok,
        out_shape=jax.ShapeDtypeStruct((nb * seq, RET_WIDTH), BF16),
        scratch_shapes=[pltpu.VMEM((RET_HEADS, RET_DK, RET_DK), F32),
                        pltpu.VMEM((RET_HEADS, RET_CHUNK, RET_CHUNK), F32),
                        pltpu.VMEM((RET_HEADS, RET_CHUNK, RET_DK), F32),
                        pltpu.VMEM((RET_HEADS, RET_CHUNK, RET_DK), F32)],
        compiler_params=_params("arbitrary", "arbitrary"),
        name="retention",
    )(q, k, v, gate, lg, gain)


def _mix_out_kernel(x_ref, ys_ref, yr_ref, wo_ref, g_ref, wq_ref, k_ref, v_ref, wco_ref, o_ref, ps_ref):
    tk = TM // S5_R
    ys = jnp.concatenate([ys_ref[:, i * S5_WIDTH:(i + 1) * S5_WIDTH] for i in range(S5_R)], axis=0)
    ps = _dot(ys, wo_ref[0:S5_WIDTH, :])
    for i in range(S5_R):
        for c in range(D_MODEL // LANES):
            ps_ref[c, pl.ds(i, tk, stride=S5_R), :] = ps[i * tk:(i + 1) * tk, c * LANES:(c + 1) * LANES]
    ssm = jnp.concatenate([ps_ref[c] for c in range(D_MODEL // LANES)], axis=1)
    x = x_ref[...] + ssm + _dot(yr_ref[...], wo_ref[S5_WIDTH:, :])
    h = _rms(x, g_ref[...]).astype(BF16)
    q = _dot(h, wq_ref[...]).astype(BF16)
    outs = []
    for hd in range(X_HEADS):
        lanes = slice(hd * X_HEAD_DIM, (hd + 1) * X_HEAD_DIM)
        s = lax.dot_general(q[:, lanes], k_ref[:, lanes], (((1,), (1,)), ((), ())),
                            preferred_element_type=F32)
        e = jnp.exp(s - jnp.max(s, axis=-1, keepdims=True))
        p = e * (1.0 / jnp.sum(e, axis=-1, keepdims=True))
        outs.append(_dot(p.astype(BF16), v_ref[:, lanes]).astype(BF16))
    o = jnp.concatenate(outs, axis=1)
    o_ref[...] = x + _dot(o, wco_ref[...])


def _mix_out(l, x2d, y_ssm, y_ret, w_out, g, wq, k_mem, v_mem, wco, tiles_per_batch):
    n = x2d.shape[0]
    row = lambda w: pl.BlockSpec((TM, w), lambda i: (i, 0))
    mem = pl.BlockSpec((None, MEM_LEN, D_MODEL), lambda i: (l, i // tiles_per_batch, 0))
    sq = _layer_spec(l, (D_MODEL, D_MODEL))
    return pl.pallas_call(
        _mix_out_kernel,
        grid=(n // TM,),
        in_specs=[row(D_MODEL), _blocked_spec(tiles_per_batch),
                  row(RET_WIDTH), sq, _layer_spec(l, (1, D_MODEL)), sq, mem, mem, sq],
        out_specs=row(D_MODEL),
        out_shape=jax.ShapeDtypeStruct((n, D_MODEL), F32),
        scratch_shapes=[pltpu.VMEM((D_MODEL // LANES, TM, LANES), F32)],
        compiler_params=_params("arbitrary"),
        name="mix_out_xattn",
    )(x2d, y_ssm, y_ret, w_out, g, wq, k_mem, v_mem, wco)


def _ffn_kernel(final, x_ref, g_ref, wg_ref, wu_ref, wd_ref, gf_ref, o_ref):
    x = x_ref[...]
    h = _rms(x, g_ref[...]).astype(BF16)
    acts = [(jax.nn.silu(_dot(h, wg_ref[:, lo:hi])) * _dot(h, wu_ref[:, lo:hi])).astype(BF16)
            for lo, hi in FF_SPLITS]
    acc = x + _dot(jnp.concatenate(acts, axis=1), wd_ref[...])
    if final:
        acc = _rms(acc, gf_ref[...])
    o_ref[...] = acc


def _ffn(l, x2d, g, wg, wu, wd, g_final, final):
    n = x2d.shape[0]
    row = pl.BlockSpec((TM, D_MODEL), lambda i: (i, 0))
    return pl.pallas_call(
        functools.partial(_ffn_kernel, final),
        grid=(n // TM,),
        in_specs=[row, _layer_spec(l, (1, D_MODEL)), _layer_spec(l, (D_MODEL, D_FF)),
                  _layer_spec(l, (D_MODEL, D_FF)), _layer_spec(l, (D_FF, D_MODEL)), _const_spec((1, D_MODEL))],
        out_specs=row,
        out_shape=jax.ShapeDtypeStruct((n, D_MODEL), F32),
        compiler_params=_params("arbitrary"),
        name="ffn_final" if final else "ffn",
    )(x2d, g, wg, wu, wd, g_final.reshape(1, D_MODEL))


def kernel(x, mem, positions, norm_mix, w_in, s5_lambda_re, s5_lambda_im, s5_log_step, s5_b_re, s5_b_im, s5_c_re, s5_c_im, s5_d, s5_w_glu, s5_b_glu, s5_out_norm, ret_out_norm, w_out, norm_cross, norm_mem, w_cq, w_ck, w_cv, w_co, norm_ffn, w_gate, w_up, w_down, norm_final):
    nb, seq, _ = x.shape
    n = nb * seq
    assert seq % RET_TR == 0 and seq % (S5_R * S5_TK) == 0 and seq % TM == 0
    bf = lambda a: a.astype(BF16)
    vec = lambda a: a.reshape(DEPTH, 1, a.shape[-1])

    cos2, sin2 = _rope_tables(positions)
    a_r, a_i, s5_bp, s5_w = _s5_discretise(
        s5_lambda_re, s5_lambda_im, s5_log_step, s5_b_re, s5_b_im, s5_c_re, s5_c_im)
    a_r = a_r.reshape(DEPTH, S5_KT, 1, S5_SLAB)
    a_i = a_i.reshape(DEPTH, S5_KT, 1, S5_SLAB)
    k_mem, v_mem = _mem_kv(mem.reshape(nb * MEM_LEN, D_MODEL), norm_mem, bf(w_ck), bf(w_cv))
    lg = jnp.log1p(-jnp.exp2(-5.0 - jnp.arange(RET_HEADS, dtype=F32)))
    lg = jnp.broadcast_to(lg[:, None, None], (RET_HEADS, 1, RET_DK))
    ret_gain = ret_out_norm.reshape(DEPTH, RET_HEADS, 1, RET_DK)
    w_in_b, w_out_b, w_cq_b, w_co_b = bf(w_in), bf(w_out), bf(w_cq * X_HEAD_DIM ** -0.5), bf(w_co)
    w_gate_b, w_up_b, w_down_b, w_glu_b = bf(w_gate), bf(w_up), bf(w_down), bf(s5_w_glu)
    g_mix, g_cross, g_ffn = vec(norm_mix), vec(norm_cross), vec(norm_ffn)
    s5_d3, s5_bglu3, s5_gout3 = vec(s5_d), vec(s5_b_glu), vec(s5_out_norm)

    x2d = x.reshape(n, D_MODEL)
    for l in range(DEPTH):
        u, ub, q, k, v, gate = _in_proj(l, x2d, g_mix, w_in_b, cos2, sin2, nb, seq)
        y_ssm = _s5_mixer(l, u, ub, s5_bp, a_r, a_i, s5_w,
                          s5_d3, w_glu_b, s5_bglu3, s5_gout3)
        y_ret = _retention(l, q, k, v, gate, lg, ret_gain, nb, seq)
        x2d = _mix_out(l, x2d, y_ssm, y_ret, w_out_b, g_cross, w_cq_b,
                       k_mem, v_mem, w_co_b, seq // TM)
        x2d = _ffn(l, x2d, g_ffn, w_gate_b, w_up_b, w_down_b, norm_final, l == DEPTH - 1)
    return x2d.reshape(nb, seq, D_MODEL)
```

```python
import functools
import math

import jax
import jax.numpy as jnp
from jax import lax
from jax.experimental import pallas as pl
from jax.experimental.pallas import tpu as pltpu

D_MODEL = 1024
DEPTH = 4
MEM_LEN = 256
S5_WIDTH = 512
S5_GROUP_CH = 16
S5_GROUPS = 32
S5_STATE = 64
RET_HEADS = 4
RET_DK = 128
RET_WIDTH = 512
IN_WIDTH = S5_WIDTH + 4 * RET_WIDTH
X_HEADS = 4
X_HEAD_DIM = 256
D_FF = 2816
ROPE_BASE = 10000.0
EPS = 1e-6

LANES = 128
SUBLANES = 8
S5_KT = S5_WIDTH // LANES
S5_GPT = LANES // S5_GROUP_CH
S5_SLAB = S5_GPT * S5_STATE
MXU_DIM = 256
S5_HG = MXU_DIM // S5_STATE
S5_KT_HALVES = S5_GPT // S5_HG
S5_R = 4
S5_BP_SHAPE = (S5_KT, S5_KT_HALVES, S5_R * S5_HG * S5_GROUP_CH, 2 * S5_HG * S5_STATE)
S5_W_SHAPE = (S5_KT, S5_KT_HALVES, (2 * S5_STATE + S5_R * S5_GROUP_CH) * S5_HG,
              S5_R * S5_HG * S5_GROUP_CH)
S5_TK = 128
S5_PITCH = S5_TK + 8
RET_CHUNK = 256
RET_TR = 2048
TM = 1024
FF_SPLITS = ((0, 1536), (1536, D_FF))

BF16 = jnp.bfloat16
F32 = jnp.float32
VMEM_LIMIT = 48 * 1024 * 1024


def _dot(a, b):
    return jnp.dot(a, b, preferred_element_type=F32)


def _rms(x, g):
    ms = jnp.mean(x * x, axis=-1, keepdims=True)
    return x * lax.rsqrt(ms + EPS) * g


def _const_spec(shape):
    nd = len(shape)
    return pl.BlockSpec(shape, lambda *_: (0,) * nd, pipeline_mode=pl.Buffered(1))


def _layer_spec(l, shape):
    nd = len(shape)
    return pl.BlockSpec((None,) + tuple(shape), lambda *_: (l,) + (0,) * nd, pipeline_mode=pl.Buffered(1))


def _params(*sem):
    return pltpu.CompilerParams(dimension_semantics=sem, vmem_limit_bytes=VMEM_LIMIT)


def _rope_kernel(pos_ref, invf_ref, cs_ref):
    low = lax.broadcasted_iota(jnp.int32, (cs_ref.shape[0], RET_DK), 1) < RET_DK // 2
    ang = jnp.where(low, pos_ref[:, 0:1], pos_ref[:, 1:2]) * invf_ref[...]
    cs_ref[:, 0:RET_DK] = jnp.cos(ang)
    cs_ref[:, RET_DK:] = jnp.sin(ang)


def _rope_tables(positions):
    n = positions.size
    half = RET_DK // 2
    inv_freq = 1.0 / (ROPE_BASE ** (jnp.arange(half, dtype=F32) / half))
    invf2 = jnp.concatenate([inv_freq, inv_freq]).reshape(1, RET_DK)
    pos = positions.astype(F32).reshape(n // TM, 2, TM // 2)
    pos = jnp.swapaxes(pos, 1, 2).reshape(n // 2, 2)
    t = 1024
    return pl.pallas_call(
        _rope_kernel,
        grid=(n // 2 // t,),
        in_specs=[pl.BlockSpec((t, 2), lambda i: (i, 0)),
                  pl.BlockSpec((1, RET_DK), lambda i: (0, 0))],
        out_specs=pl.BlockSpec((t, 2 * RET_DK), lambda i: (i, 0)),
        out_shape=jax.ShapeDtypeStruct((n // 2, 2 * RET_DK), F32),
        compiler_params=_params("arbitrary"),
        name="rope_tables",
    )(pos, invf2)


def _unpack_rope(cs_ref):
    half = RET_DK // 2
    c, s = cs_ref[:, 0:RET_DK], cs_ref[:, RET_DK:]
    low = lax.broadcasted_iota(jnp.int32, c.shape, 1) < half
    c_sw, s_sw = pltpu.roll(c, half, axis=1), pltpu.roll(s, half, axis=1)
    cos = jnp.concatenate([jnp.where(low, c, c_sw), jnp.where(low, c_sw, c)], axis=0)
    sin = jnp.concatenate([jnp.where(low, -s, s_sw), jnp.where(low, -s_sw, s)], axis=0)
    return cos, sin


def _tile_lanes(x, reps):
    while reps > 1:
        x = jnp.concatenate([x, x], axis=1)
        reps //= 2
    return x


def _block_diag(blocks, kt):
    _, r, c = blocks.shape
    x = _tile_lanes(blocks[kt * S5_GPT:(kt + 1) * S5_GPT].reshape(S5_GPT * r, c), S5_GPT)
    row_group = lax.broadcasted_iota(jnp.int32, x.shape, 0) // r
    col_group = lax.broadcasted_iota(jnp.int32, x.shape, 1) // c
    return jnp.where(row_group == col_group, x, 0.0)


def _s5_disc_kernel(lr_ref, li_ref, ls_ref, brt_ref, bit_ref, cr_ref, ci_ref, ar_ref, ai_ref, bp_ref, w_ref):
    lr = lr_ref[0]
    li = li_ref[0]
    step = jnp.exp(ls_ref[0])
    mag = jnp.exp(lr * step)
    a_r = mag * jnp.cos(li * step)
    a_i = mag * jnp.sin(li * step)
    den = lr * lr + li * li
    f_r = ((a_r - 1.0) * lr + a_i * li) / den
    f_i = (a_i * lr - (a_r - 1.0) * li) / den
    br = brt_ref[0]
    bi = bit_ref[0]
    bb_r = f_r * br - f_i * bi
    bb_i = f_r * bi + f_i * br
    cr = cr_ref[0]
    ci = ci_ref[0]
    contract_p = (((2,), (2,)), ((0,), (0,)))
    hi = dict(precision=lax.Precision.HIGHEST, preferred_element_type=F32)
    hc, hs = S5_HG * S5_GROUP_CH, S5_HG * S5_STATE
    pw_r = jnp.ones_like(a_r)
    pw_i = jnp.zeros_like(a_i)
    ro_r, ro_i, lags = [], [], []
    for m in range(S5_R):
        abb_r = pw_r * bb_r - pw_i * bb_i
        abb_i = pw_r * bb_i + pw_i * bb_r
        lag = lax.dot_general(abb_r, cr, contract_p, **hi) - lax.dot_general(abb_i, ci, contract_p, **hi)
        pw_r, pw_i = pw_r * a_r - pw_i * a_i, pw_r * a_i + pw_i * a_r
        cp_r = cr * pw_r - ci * pw_i
        cp_i = cr * pw_i + ci * pw_r
        i = S5_R - 1 - m
        for kt in range(S5_KT):
            bd_r = _block_diag(abb_r, kt)
            bd_i = _block_diag(abb_i, kt)
            for half in range(S5_KT_HALVES):
                rs, cs = slice(half * hc, (half + 1) * hc), slice(half * hs, (half + 1) * hs)
                bp_ref[0, kt, half, i * hc:(i + 1) * hc, :] = jnp.concatenate(
                    [bd_r[rs, cs], bd_i[rs, cs]], axis=1).astype(BF16)
        ro_r.append([_block_diag(cp_r, kt).T for kt in range(S5_KT)])
        ro_i.append([(-_block_diag(cp_i, kt)).T for kt in range(S5_KT)])
        lags.append([_block_diag(lag, kt) for kt in range(S5_KT)])
    zero = jnp.zeros((hc, hc), F32)
    for kt in range(S5_KT):
        for half in range(S5_KT_HALVES):
            rs, cs = slice(half * hs, (half + 1) * hs), slice(half * hc, (half + 1) * hc)
            w_ref[0, kt, half, 0:hs, :] = jnp.concatenate(
                [ro_r[j][kt][rs, cs] for j in range(S5_R)], axis=1).astype(BF16)
            w_ref[0, kt, half, hs:2 * hs, :] = jnp.concatenate(
                [ro_i[j][kt][rs, cs] for j in range(S5_R)], axis=1).astype(BF16)
            for i in range(S5_R):
                w_ref[0, kt, half, 2 * hs + i * hc:2 * hs + (i + 1) * hc, :] = jnp.concatenate(
                    [lags[j - i][kt][cs, cs] if j >= i else zero for j in range(S5_R)], axis=1).astype(BF16)
    ar_ref[0] = pw_r
    ai_ref[0] = pw_i


def _s5_discretise(lam_re, lam_im, log_step, b_re, b_im, c_re, c_im):
    g, p, h = S5_GROUPS, S5_STATE, S5_GROUP_CH
    lr = lam_re.reshape(DEPTH, g, 1, p)
    li = lam_im.reshape(DEPTH, g, 1, p)
    ls = log_step.reshape(DEPTH, g, 1, 1)
    brt = jnp.swapaxes(b_re, -1, -2)
    bit = jnp.swapaxes(b_im, -1, -2)
    vec = pl.BlockSpec((1, g, 1, p), lambda l: (l, 0, 0, 0))
    mat = pl.BlockSpec((1, g, h, p), lambda l: (l, 0, 0, 0))
    return pl.pallas_call(
        _s5_disc_kernel,
        grid=(DEPTH,),
        in_specs=[vec, vec, pl.BlockSpec((1, g, 1, 1), lambda l: (l, 0, 0, 0)), mat, mat, mat, mat],
        out_specs=[vec, vec, pl.BlockSpec((1,) + S5_BP_SHAPE, lambda l: (l, 0, 0, 0, 0)),
                   pl.BlockSpec((1,) + S5_W_SHAPE, lambda l: (l, 0, 0, 0, 0))],
        out_shape=[jax.ShapeDtypeStruct((DEPTH, g, 1, p), F32)] * 2
        + [jax.ShapeDtypeStruct((DEPTH,) + S5_BP_SHAPE, BF16), jax.ShapeDtypeStruct((DEPTH,) + S5_W_SHAPE, BF16)],
        compiler_params=_params("arbitrary"),
        name="s5_discretise",
    )(lr, li, ls, brt, bit, c_re, c_im)


def _memkv_kernel(mem_ref, g_ref, wk_ref, wv_ref, k_ref, v_ref):
    m = _rms(mem_ref[...], g_ref[0]).astype(BF16)
    k_ref[0] = _dot(m, wk_ref[0]).astype(BF16)
    v_ref[0] = _dot(m, wv_ref[0]).astype(BF16)


def _mem_kv(mem2d, norm_mem, w_ck, w_cv):
    rows = mem2d.shape[0]
    wspec = pl.BlockSpec((1, D_MODEL, D_MODEL), lambda l: (l, 0, 0))
    ospec = pl.BlockSpec((1, rows, D_MODEL), lambda l: (l, 0, 0))
    return pl.pallas_call(
        _memkv_kernel,
        grid=(DEPTH,),
        in_specs=[pl.BlockSpec((rows, D_MODEL), lambda l: (0, 0)),
                  pl.BlockSpec((1, 1, D_MODEL), lambda l: (l, 0, 0)), wspec, wspec],
        out_specs=[ospec, ospec],
        out_shape=[jax.ShapeDtypeStruct((DEPTH, rows, D_MODEL), BF16)] * 2,
        compiler_params=_params("arbitrary"),
        name="mem_kv",
    )(mem2d, norm_mem.reshape(DEPTH, 1, D_MODEL), w_ck, w_cv)


def _in_proj_kernel(x_ref, g_ref, w_ref, cs_ref, u_ref, ub_ref, qkv_ref, gate_ref, us_ref):
    h = _rms(x_ref[...], g_ref[...]).astype(BF16)
    cos, sin = _unpack_rope(cs_ref)
    u = _dot(h, w_ref[:, 0:S5_WIDTH])
    for c in range(S5_KT):
        us_ref[c] = u[:, c * LANES:(c + 1) * LANES]
    hc = S5_HG * S5_GROUP_CH
    for c in range(S5_KT):
        pieces = [us_ref[c, pl.ds(i, TM // S5_R, stride=S5_R), :] for i in range(S5_R)]
        for i in range(S5_R):
            u_ref[:, i * S5_WIDTH + c * LANES:i * S5_WIDTH + (c + 1) * LANES] = pieces[i]
        for half in range(S5_KT_HALVES):
            cols = jnp.concatenate([p[:, half * hc:(half + 1) * hc] for p in pieces], axis=1)
            base = (c * S5_KT_HALVES + half) * S5_R * hc
            ub_ref[:, base:base + S5_R * hc] = cols.astype(BF16)
    for dst, off, scale in ((0, S5_WIDTH, None), (RET_WIDTH, S5_WIDTH + RET_WIDTH, RET_DK ** -0.5)):
        t = _dot(h, w_ref[:, off:off + RET_WIDTH])
        for hd in range(RET_HEADS):
            th = t[:, hd * RET_DK:(hd + 1) * RET_DK]
            r = th * cos + pltpu.roll(th, RET_DK // 2, axis=1) * sin
            if scale is not None:
                r = r * scale
            qkv_ref[:, dst + hd * RET_DK:dst + (hd + 1) * RET_DK] = r.astype(BF16)
    off = S5_WIDTH + 2 * RET_WIDTH
    qkv_ref[:, 2 * RET_WIDTH:] = _dot(h, w_ref[:, off:off + RET_WIDTH]).astype(BF16)
    gate_ref[...] = _dot(h, w_ref[:, off + RET_WIDTH:off + 2 * RET_WIDTH])


def _blocked_spec(tiles_per_batch):
    return pl.BlockSpec((None, TM // S5_R, S5_R * S5_WIDTH),
                        lambda i: (i // tiles_per_batch, i % tiles_per_batch, 0))


def _in_proj(l, x2d, g, w_bf16, rope, nb, seq):
    n = x2d.shape[0]
    row = lambda w: pl.BlockSpec((TM, w), lambda i: (i, 0))
    return pl.pallas_call(
        _in_proj_kernel,
        grid=(n // TM,),
        in_specs=[row(D_MODEL), _layer_spec(l, (1, D_MODEL)), _layer_spec(l, (D_MODEL, IN_WIDTH)),
                  pl.BlockSpec((TM // 2, 2 * RET_DK), lambda i: (i, 0))],
        out_specs=[_blocked_spec(seq // TM), _blocked_spec(seq // TM), row(3 * RET_WIDTH), row(RET_WIDTH)],
        out_shape=[jax.ShapeDtypeStruct((nb, seq // S5_R, S5_R * S5_WIDTH), F32),
                   jax.ShapeDtypeStruct((nb, seq // S5_R, S5_R * S5_WIDTH), BF16),
                   jax.ShapeDtypeStruct((n, 3 * RET_WIDTH), BF16),
                   jax.ShapeDtypeStruct((n, RET_WIDTH), F32)],
        scratch_shapes=[pltpu.VMEM((S5_KT, TM, LANES), F32)],
        compiler_params=_params("arbitrary"),
        name="in_proj",
    )(x2d, g, w_bf16, rope)


def _s5_kernel(nb, u_ref, ub_ref, bp_ref, ar_ref, ai_ref, w_ref, d_ref, wglu_ref, bglu_ref, gout_ref,
               o_ref, st_ref, y_ref, *xs_refs):
    nslab = S5_SLAB // LANES
    rows = nb * S5_TK

    @pl.when(pl.program_id(0) == 0)
    def _():
        st_ref[...] = jnp.zeros_like(st_ref)

    pack = SUBLANES // nb
    npk = nslab // pack
    hc, hs = S5_HG * S5_GROUP_CH, S5_HG * S5_STATE
    half_slabs = hs // LANES

    def slab_block(s, b):
        return s // pack, ((s % pack) * nb + b) * S5_PITCH

    def half_state_slabs(half):
        re = [half * half_slabs + j for j in range(half_slabs)]
        return re + [nslab + s for s in re]

    def block_inputs(kt, half):
        base = (kt * S5_KT_HALVES + half) * S5_R * hc
        return ub_ref[:, :, base:base + S5_R * hc].reshape(rows, S5_R * hc)

    def state_input(kt, half):
        xb = _dot(block_inputs(kt, half), bp_ref[kt, half])
        for b in range(nb):
            for jj, s in enumerate(half_state_slabs(half)):
                p, r0 = slab_block(s, b)
                xs_refs[kt][p, r0:r0 + S5_TK, :] = xb[b * S5_TK:(b + 1) * S5_TK, jj * LANES:(jj + 1) * LANES]

    def readout(kt):
        def slab(s, b):
            p, r0 = slab_block(s, b)
            return xs_refs[kt][p, r0:r0 + S5_TK, :]

        yk = []
        for half in range(S5_KT_HALVES):
            s_prev = jnp.concatenate(
                [jnp.concatenate([slab(s, b) for s in half_state_slabs(half)], axis=1) for b in range(nb)],
                axis=0).astype(BF16)
            yk.append(_dot(jnp.concatenate([s_prev, block_inputs(kt, half)], axis=1), w_ref[kt, half]))
        for j in range(S5_R):
            at = slice(j * S5_WIDTH + kt * LANES, j * S5_WIDTH + (kt + 1) * LANES)
            y = (jnp.concatenate([h[:, j * hc:(j + 1) * hc] for h in yk], axis=1).reshape(nb, S5_TK, LANES)
                 + d_ref[:, kt * LANES:(kt + 1) * LANES] * u_ref[:, :, at])
            y_ref[:, :, at] = 0.5 * y * (1.0 + lax.erf(y * math.sqrt(0.5)))

    def recurrence(kt):
        def packed(ref, q):
            return jnp.concatenate(
                [jnp.broadcast_to(ref[kt, :, s * LANES:(s + 1) * LANES], (nb, LANES))
                 for s in range(q * pack, (q + 1) * pack)], axis=0)

        a_r = [packed(ar_ref, q) for q in range(npk)]
        a_i = [packed(ai_ref, q) for q in range(npk)]
        s_r = [st_ref[kt, q] for q in range(npk)]
        s_i = [st_ref[kt, npk + q] for q in range(npk)]
        for t in range(S5_TK):
            at = pl.ds(t, pack * nb, stride=S5_PITCH)
            for q in range(npk):
                x_r = xs_refs[kt][q, at, :]
                x_i = xs_refs[kt][npk + q, at, :]
                xs_refs[kt][q, at, :] = s_r[q]
                xs_refs[kt][npk + q, at, :] = s_i[q]
                s_r[q], s_i[q] = (a_r[q] * s_r[q] - a_i[q] * s_i[q] + x_r,
                                  a_r[q] * s_i[q] + a_i[q] * s_r[q] + x_i)
        for q in range(npk):
            st_ref[kt, q] = s_r[q]
            st_ref[kt, npk + q] = s_i[q]

    for kt in range(S5_KT):
        for half in range(S5_KT_HALVES):
            state_input(kt, half)
    for kt in range(S5_KT):
        recurrence(kt)
        readout(kt)

    for i in range(S5_R):
        cols = slice(i * S5_WIDTH, (i + 1) * S5_WIDTH)
        z = y_ref[:, :, cols].reshape(rows, S5_WIDTH)
        z = z * jax.nn.sigmoid(_dot(z.astype(BF16), wglu_ref[...]) + bglu_ref[...])
        o_ref[:, :, cols] = _rms(z, gout_ref[...]).astype(BF16).reshape(nb, S5_TK, S5_WIDTH)


def _s5_mixer(l, u3, ub3, bp, a_r, a_i, w, d, wglu, bglu, gout):
    nb, krows, width = u3.shape
    assert SUBLANES % nb == 0
    nbuf = 2 * S5_SLAB // LANES * nb // SUBLANES
    tok = pl.BlockSpec((nb, S5_TK, width), lambda c: (0, c, 0))
    return pl.pallas_call(
        functools.partial(_s5_kernel, nb),
        grid=(krows // S5_TK,),
        in_specs=[tok, tok,
                  _layer_spec(l, S5_BP_SHAPE),
                  _layer_spec(l, (S5_KT, 1, S5_SLAB)), _layer_spec(l, (S5_KT, 1, S5_SLAB)),
                  _layer_spec(l, S5_W_SHAPE),
                  _layer_spec(l, (1, S5_WIDTH)), _layer_spec(l, (S5_WIDTH, S5_WIDTH)),
                  _layer_spec(l, (1, S5_WIDTH)), _layer_spec(l, (1, S5_WIDTH))],
        out_specs=tok,
        out_shape=jax.ShapeDtypeStruct((nb, krows, width), BF16),
        scratch_shapes=[pltpu.VMEM((S5_KT, nbuf, SUBLANES, LANES), F32),
                        pltpu.VMEM((nb, S5_TK, width), F32)]
        + [pltpu.VMEM((nbuf, SUBLANES * S5_PITCH, LANES), F32)] * S5_KT,
        compiler_params=_params("arbitrary"),
        name="s5_mixer",
    )(u3, ub3, bp, a_r, a_i, w, d, wglu, bglu, gout)


def _ret_kernel(qkv_ref, g_ref, lg_ref, gain_ref, o_ref, st_ref, dm_ref, zeta_ref, xi_ref):
    @pl.when((pl.program_id(0) == 0) & (pl.program_id(1) == 0))
    def _():
        row = lax.broadcasted_iota(jnp.int32, (RET_CHUNK, RET_CHUNK), 0).astype(F32)
        col = lax.broadcasted_iota(jnp.int32, (RET_CHUNK, RET_CHUNK), 1).astype(F32)
        diff = row - col
        idx = lax.broadcasted_iota(jnp.int32, (RET_CHUNK, RET_DK), 0).astype(F32)
        for hd in range(RET_HEADS):
            lg = lg_ref[hd]
            lg_wide = jnp.concatenate([lg] * (RET_CHUNK // RET_DK), axis=1)
            dm_ref[hd] = jnp.where(diff >= 0, jnp.exp(jnp.maximum(diff, 0.0) * lg_wide), 0.0)
            zeta_ref[hd] = jnp.exp((RET_CHUNK - 1.0 - idx) * lg)
            xi_ref[hd] = jnp.exp((idx + 1.0) * lg)

    @pl.when(pl.program_id(1) == 0)
    def _():
        st_ref[...] = jnp.zeros_like(st_ref)

    for c in range(RET_TR // RET_CHUNK):
        rows = slice(c * RET_CHUNK, (c + 1) * RET_CHUNK)
        for hd in range(RET_HEADS):
            lanes = slice(hd * RET_DK, (hd + 1) * RET_DK)
            qh, kh, vh = (qkv_ref[rows, part * RET_WIDTH + hd * RET_DK:part * RET_WIDTH + (hd + 1) * RET_DK]
                          for part in range(3))
            s = lax.dot_general(qh, kh, (((1,), (1,)), ((), ())), preferred_element_type=F32)
            inner = _dot((s * dm_ref[hd]).astype(BF16), vh)
            state = st_ref[hd]
            cross = _dot(qh, state.astype(BF16)) * xi_ref[hd]
            kz = (kh.astype(F32) * zeta_ref[hd]).astype(BF16)
            kv = lax.dot_general(kz, vh, (((0,), (0,)), ((), ())), preferred_element_type=F32)
            st_ref[hd] = jnp.exp(RET_CHUNK * lg_ref[hd]) * state + kv
            o = inner + cross
            mu = jnp.mean(o, axis=-1, keepdims=True)
            oc = o - mu
            var = jnp.mean(oc * oc, axis=-1, keepdims=True)
            o = oc * lax.rsqrt(var + EPS) * gain_ref[hd]
            o_ref[rows, lanes] = (o * jax.nn.silu(g_ref[rows, lanes])).astype(BF16)


def _retention(l, qkv, gate, lg, gain, nb, seq):
    steps = seq // RET_TR
    tok = pl.BlockSpec((RET_TR, RET_WIDTH), lambda b, c: (b * steps + c, 0))
    return pl.pallas_call(
        _ret_kernel,
        grid=(nb, steps),
        in_specs=[pl.BlockSpec((RET_TR, 3 * RET_WIDTH), lambda b, c: (b * steps + c, 0)), tok,
                  _const_spec((RET_HEADS, 1, RET_DK)), _layer_spec(l, (RET_HEADS, 1, RET_DK))],
        out_specs=tok,
        out_shape=jax.ShapeDtypeStruct((nb * seq, RET_WIDTH), BF16),
        scratch_shapes=[pltpu.VMEM((RET_HEADS, RET_DK, RET_DK), F32),
                        pltpu.VMEM((RET_HEADS, RET_CHUNK, RET_CHUNK), F32),
                        pltpu.VMEM((RET_HEADS, RET_CHUNK, RET_DK), F32),
                        pltpu.VMEM((RET_HEADS, RET_CHUNK, RET_DK), F32)],
        compiler_params=_params("arbitrary", "arbitrary"),
        name="retention",
    )(qkv, gate, lg, gain)


def _mix_out_kernel(x_ref, ys_ref, yr_ref, wo_ref, g_ref, wq_ref, k_ref, v_ref, wco_ref, o_ref, ps_ref):
    tk = TM // S5_R
    ys = jnp.concatenate([ys_ref[:, i * S5_WIDTH:(i + 1) * S5_WIDTH] for i in range(S5_R)], axis=0)
    ps = _dot(ys, wo_ref[0:S5_WIDTH, :])
    for i in range(S5_R):
        for c in range(D_MODEL // LANES):
            ps_ref[c, pl.ds(i, tk, stride=S5_R), :] = ps[i * tk:(i + 1) * tk, c * LANES:(c + 1) * LANES]
    ssm = jnp.concatenate([ps_ref[c] for c in range(D_MODEL // LANES)], axis=1)
    x = x_ref[...] + ssm + _dot(yr_ref[...], wo_ref[S5_WIDTH:, :])
    h = _rms(x, g_ref[...]).astype(BF16)
    q = _dot(h, wq_ref[...]).astype(BF16)
    outs = []
    for hd in range(X_HEADS):
        lanes = slice(hd * X_HEAD_DIM, (hd + 1) * X_HEAD_DIM)
        s = lax.dot_general(q[:, lanes], k_ref[:, lanes], (((1,), (1,)), ((), ())),
                            preferred_element_type=F32)
        e = jnp.exp(s - jnp.max(s, axis=-1, keepdims=True))
        p = e * (1.0 / jnp.sum(e, axis=-1, keepdims=True))
        outs.append(_dot(p.astype(BF16), v_ref[:, lanes]).astype(BF16))
    o = jnp.concatenate(outs, axis=1)
    o_ref[...] = x + _dot(o, wco_ref[...])


def _mix_out(l, x2d, y_ssm, y_ret, w_out, g, wq, k_mem, v_mem, wco, tiles_per_batch):
    n = x2d.shape[0]
    row = lambda w: pl.BlockSpec((TM, w), lambda i: (i, 0))
    mem = pl.BlockSpec((None, MEM_LEN, D_MODEL), lambda i: (l, i // tiles_per_batch, 0))
    sq = _layer_spec(l, (D_MODEL, D_MODEL))
    return pl.pallas_call(
        _mix_out_kernel,
        grid=(n // TM,),
        in_specs=[row(D_MODEL), _blocked_spec(tiles_per_batch),
                  row(RET_WIDTH), sq, _layer_spec(l, (1, D_MODEL)), sq, mem, mem, sq],
        out_specs=row(D_MODEL),
        out_shape=jax.ShapeDtypeStruct((n, D_MODEL), F32),
        scratch_shapes=[pltpu.VMEM((D_MODEL // LANES, TM, LANES), F32)],
        compiler_params=_params("arbitrary"),
        name="mix_out_xattn",
    )(x2d, y_ssm, y_ret, w_out, g, wq, k_mem, v_mem, wco)


def _ffn_kernel(final, x_ref, g_ref, wg_ref, wu_ref, wd_ref, gf_ref, o_ref):
    x = x_ref[...]
    h = _rms(x, g_ref[...]).astype(BF16)
    acts = [(jax.nn.silu(_dot(h, wg_ref[:, lo:hi])) * _dot(h, wu_ref[:, lo:hi])).astype(BF16)
            for lo, hi in FF_SPLITS]
    acc = x + _dot(jnp.concatenate(acts, axis=1), wd_ref[...])
    if final:
        acc = _rms(acc, gf_ref[...])
    o_ref[...] = acc


def _ffn(l, x2d, g, wg, wu, wd, g_final, final):
    n = x2d.shape[0]
    row = pl.BlockSpec((TM, D_MODEL), lambda i: (i, 0))
    return pl.pallas_call(
        functools.partial(_ffn_kernel, final),
        grid=(n // TM,),
        in_specs=[row, _layer_spec(l, (1, D_MODEL)), _layer_spec(l, (D_MODEL, D_FF)),
                  _layer_spec(l, (D_MODEL, D_FF)), _layer_spec(l, (D_FF, D_MODEL)), _const_spec((1, D_MODEL))],
        out_specs=row,
        out_shape=jax.ShapeDtypeStruct((n, D_MODEL), F32),
        compiler_params=_params("arbitrary"),
        name="ffn_final" if final else "ffn",
    )(x2d, g, wg, wu, wd, g_final.reshape(1, D_MODEL))


def kernel(x, mem, positions, norm_mix, w_in, s5_lambda_re, s5_lambda_im, s5_log_step, s5_b_re, s5_b_im, s5_c_re, s5_c_im, s5_d, s5_w_glu, s5_b_glu, s5_out_norm, ret_out_norm, w_out, norm_cross, norm_mem, w_cq, w_ck, w_cv, w_co, norm_ffn, w_gate, w_up, w_down, norm_final):
    nb, seq, _ = x.shape
    n = nb * seq
    assert seq % RET_TR == 0 and seq % (S5_R * S5_TK) == 0 and seq % TM == 0
    bf = lambda a: a.astype(BF16)
    vec = lambda a: a.reshape(DEPTH, 1, a.shape[-1])

    rope = _rope_tables(positions)
    a_r, a_i, s5_bp, s5_w = _s5_discretise(
        s5_lambda_re, s5_lambda_im, s5_log_step, s5_b_re, s5_b_im, s5_c_re, s5_c_im)
    a_r = a_r.reshape(DEPTH, S5_KT, 1, S5_SLAB)
    a_i = a_i.reshape(DEPTH, S5_KT, 1, S5_SLAB)
    k_mem, v_mem = _mem_kv(mem.reshape(nb * MEM_LEN, D_MODEL), norm_mem, bf(w_ck), bf(w_cv))
    lg = jnp.log1p(-jnp.exp2(-5.0 - jnp.arange(RET_HEADS, dtype=F32)))
    lg = jnp.broadcast_to(lg[:, None, None], (RET_HEADS, 1, RET_DK))
    ret_gain = ret_out_norm.reshape(DEPTH, RET_HEADS, 1, RET_DK)
    w_in_b, w_out_b, w_cq_b, w_co_b = bf(w_in), bf(w_out), bf(w_cq * X_HEAD_DIM ** -0.5), bf(w_co)
    w_gate_b, w_up_b, w_down_b, w_glu_b = bf(w_gate), bf(w_up), bf(w_down), bf(s5_w_glu)
    g_mix, g_cross, g_ffn = vec(norm_mix), vec(norm_cross), vec(norm_ffn)
    s5_d3, s5_bglu3, s5_gout3 = vec(s5_d), vec(s5_b_glu), vec(s5_out_norm)

    x2d = x.reshape(n, D_MODEL)
    for l in range(DEPTH):
        u, ub, qkv, gate = _in_proj(l, x2d, g_mix, w_in_b, rope, nb, seq)
        y_ssm = _s5_mixer(l, u, ub, s5_bp, a_r, a_i, s5_w,
                          s5_d3, w_glu_b, s5_bglu3, s5_gout3)
        y_ret = _retention(l, qkv, gate, lg, ret_gain, nb, seq)
        x2d = _mix_out(l, x2d, y_ssm, y_ret, w_out_b, g_cross, w_cq_b,
                       k_mem, v_mem, w_co_b, seq // TM)
        x2d = _ffn(l, x2d, g_ffn, w_gate_b, w_up_b, w_down_b, norm_final, l == DEPTH - 1)
    return x2d.reshape(nb, seq, D_MODEL)
```

```python
import functools
import math

import jax
import jax.numpy as jnp
from jax import lax
from jax.experimental import pallas as pl
from jax.experimental.pallas import tpu as pltpu

D_MODEL = 1024
DEPTH = 4
MEM_LEN = 256
S5_WIDTH = 512
S5_GROUP_CH = 16
S5_GROUPS = 32
S5_STATE = 64
RET_HEADS = 4
RET_DK = 128
RET_WIDTH = 512
IN_WIDTH = S5_WIDTH + 4 * RET_WIDTH
X_HEADS = 4
X_HEAD_DIM = 256
D_FF = 2816
ROPE_BASE = 10000.0
EPS = 1e-6

LANES = 128
SUBLANES = 8
S5_KT = S5_WIDTH // LANES
S5_GPT = LANES // S5_GROUP_CH
S5_SLAB = S5_GPT * S5_STATE
MXU_DIM = 256
S5_HG = MXU_DIM // S5_STATE
S5_KT_HALVES = S5_GPT // S5_HG
S5_R = 4
S5_BP_SHAPE = (S5_KT, S5_KT_HALVES, S5_R * S5_HG * S5_GROUP_CH, 2 * S5_HG * S5_STATE)
S5_W_SHAPE = (S5_KT, S5_KT_HALVES, (2 * S5_STATE + S5_R * S5_GROUP_CH) * S5_HG,
              S5_R * S5_HG * S5_GROUP_CH)
S5_TK = 128
S5_PITCH = S5_TK + 8
RET_CHUNK = 256
RET_TR = 2048
TM = 1024
FF_SPLITS = ((0, 1536), (1536, D_FF))

BF16 = jnp.bfloat16
F32 = jnp.float32
VMEM_LIMIT = 48 * 1024 * 1024
SINGLE_BUFFER_MIN_ELEMS = 64 * 1024


def _dot(a, b):
    return jnp.dot(a, b, preferred_element_type=F32)


def _rms(x, g):
    ms = jnp.mean(x * x, axis=-1, keepdims=True)
    return x * lax.rsqrt(ms + EPS) * g


def _resident(shape):
    return dict(pipeline_mode=pl.Buffered(1)) if math.prod(shape) >= SINGLE_BUFFER_MIN_ELEMS else {}


def _const_spec(shape):
    nd = len(shape)
    return pl.BlockSpec(shape, lambda *_: (0,) * nd, **_resident(shape))


def _layer_spec(l, shape):
    nd = len(shape)
    return pl.BlockSpec((None,) + tuple(shape), lambda *_: (l,) + (0,) * nd, **_resident(shape))


def _params(*sem):
    return pltpu.CompilerParams(dimension_semantics=sem, vmem_limit_bytes=VMEM_LIMIT)


def _rope_kernel(pos_ref, invf_ref, cs_ref):
    low = lax.broadcasted_iota(jnp.int32, (cs_ref.shape[0], RET_DK), 1) < RET_DK // 2
    ang = jnp.where(low, pos_ref[:, 0:1], pos_ref[:, 1:2]) * invf_ref[...]
    cs_ref[:, 0:RET_DK] = jnp.cos(ang)
    cs_ref[:, RET_DK:] = jnp.sin(ang)


def _rope_tables(positions):
    n = positions.size
    half = RET_DK // 2
    inv_freq = 1.0 / (ROPE_BASE ** (jnp.arange(half, dtype=F32) / half))
    invf2 = jnp.concatenate([inv_freq, inv_freq]).reshape(1, RET_DK)
    pos = positions.astype(F32).reshape(n // TM, 2, TM // 2)
    pos = jnp.swapaxes(pos, 1, 2).reshape(n // 2, 2)
    t = 1024
    return pl.pallas_call(
        _rope_kernel,
        grid=(n // 2 // t,),
        in_specs=[pl.BlockSpec((t, 2), lambda i: (i, 0)),
                  pl.BlockSpec((1, RET_DK), lambda i: (0, 0))],
        out_specs=pl.BlockSpec((t, 2 * RET_DK), lambda i: (i, 0)),
        out_shape=jax.ShapeDtypeStruct((n // 2, 2 * RET_DK), F32),
        compiler_params=_params("arbitrary"),
        name="rope_tables",
    )(pos, invf2)


def _unpack_rope(cs_ref):
    half = RET_DK // 2
    c, s = cs_ref[:, 0:RET_DK], cs_ref[:, RET_DK:]
    low = lax.broadcasted_iota(jnp.int32, c.shape, 1) < half
    c_sw, s_sw = pltpu.roll(c, half, axis=1), pltpu.roll(s, half, axis=1)
    cos = jnp.concatenate([jnp.where(low, c, c_sw), jnp.where(low, c_sw, c)], axis=0)
    sin = jnp.concatenate([jnp.where(low, -s, s_sw), jnp.where(low, -s_sw, s)], axis=0)
    return cos, sin


def _tile_lanes(x, reps):
    while reps > 1:
        x = jnp.concatenate([x, x], axis=1)
        reps //= 2
    return x


def _block_diag(blocks, kt):
    _, r, c = blocks.shape
    x = _tile_lanes(blocks[kt * S5_GPT:(kt + 1) * S5_GPT].reshape(S5_GPT * r, c), S5_GPT)
    row_group = lax.broadcasted_iota(jnp.int32, x.shape, 0) // r
    col_group = lax.broadcasted_iota(jnp.int32, x.shape, 1) // c
    return jnp.where(row_group == col_group, x, 0.0)


def _s5_disc_kernel(lr_ref, li_ref, ls_ref, brt_ref, bit_ref, cr_ref, ci_ref, ar_ref, ai_ref, bp_ref, w_ref):
    lr = lr_ref[0]
    li = li_ref[0]
    step = jnp.exp(ls_ref[0])
    mag = jnp.exp(lr * step)
    a_r = mag * jnp.cos(li * step)
    a_i = mag * jnp.sin(li * step)
    den = lr * lr + li * li
    f_r = ((a_r - 1.0) * lr + a_i * li) / den
    f_i = (a_i * lr - (a_r - 1.0) * li) / den
    br = brt_ref[0]
    bi = bit_ref[0]
    bb_r = f_r * br - f_i * bi
    bb_i = f_r * bi + f_i * br
    cr = cr_ref[0]
    ci = ci_ref[0]
    contract_p = (((2,), (2,)), ((0,), (0,)))
    hi = dict(precision=lax.Precision.HIGHEST, preferred_element_type=F32)
    hc, hs = S5_HG * S5_GROUP_CH, S5_HG * S5_STATE
    pw_r = jnp.ones_like(a_r)
    pw_i = jnp.zeros_like(a_i)
    ro_r, ro_i, lags = [], [], []
    for m in range(S5_R):
        abb_r = pw_r * bb_r - pw_i * bb_i
        abb_i = pw_r * bb_i + pw_i * bb_r
        lag = lax.dot_general(abb_r, cr, contract_p, **hi) - lax.dot_general(abb_i, ci, contract_p, **hi)
        pw_r, pw_i = pw_r * a_r - pw_i * a_i, pw_r * a_i + pw_i * a_r
        cp_r = cr * pw_r - ci * pw_i
        cp_i = cr * pw_i + ci * pw_r
        i = S5_R - 1 - m
        for kt in range(S5_KT):
            bd_r = _block_diag(abb_r, kt)
            bd_i = _block_diag(abb_i, kt)
            for half in range(S5_KT_HALVES):
                rs, cs = slice(half * hc, (half + 1) * hc), slice(half * hs, (half + 1) * hs)
                bp_ref[0, kt, half, i * hc:(i + 1) * hc, :] = jnp.concatenate(
                    [bd_r[rs, cs], bd_i[rs, cs]], axis=1).astype(BF16)
        ro_r.append([_block_diag(cp_r, kt).T for kt in range(S5_KT)])
        ro_i.append([(-_block_diag(cp_i, kt)).T for kt in range(S5_KT)])
        lags.append([_block_diag(lag, kt) for kt in range(S5_KT)])
    zero = jnp.zeros((hc, hc), F32)
    for kt in range(S5_KT):
        for half in range(S5_KT_HALVES):
            rs, cs = slice(half * hs, (half + 1) * hs), slice(half * hc, (half + 1) * hc)
            w_ref[0, kt, half, 0:hs, :] = jnp.concatenate(
                [ro_r[j][kt][rs, cs] for j in range(S5_R)], axis=1).astype(BF16)
            w_ref[0, kt, half, hs:2 * hs, :] = jnp.concatenate(
                [ro_i[j][kt][rs, cs] for j in range(S5_R)], axis=1).astype(BF16)
            for i in range(S5_R):
                w_ref[0, kt, half, 2 * hs + i * hc:2 * hs + (i + 1) * hc, :] = jnp.concatenate(
                    [lags[j - i][kt][cs, cs] if j >= i else zero for j in range(S5_R)], axis=1).astype(BF16)
    ar_ref[0] = pw_r
    ai_ref[0] = pw_i


def _s5_discretise(lam_re, lam_im, log_step, b_re, b_im, c_re, c_im):
    g, p, h = S5_GROUPS, S5_STATE, S5_GROUP_CH
    lr = lam_re.reshape(DEPTH, g, 1, p)
    li = lam_im.reshape(DEPTH, g, 1, p)
    ls = log_step.reshape(DEPTH, g, 1, 1)
    brt = jnp.swapaxes(b_re, -1, -2)
    bit = jnp.swapaxes(b_im, -1, -2)
    vec = pl.BlockSpec((1, g, 1, p), lambda l: (l, 0, 0, 0))
    mat = pl.BlockSpec((1, g, h, p), lambda l: (l, 0, 0, 0))
    return pl.pallas_call(
        _s5_disc_kernel,
        grid=(DEPTH,),
        in_specs=[vec, vec, pl.BlockSpec((1, g, 1, 1), lambda l: (l, 0, 0, 0)), mat, mat, mat, mat],
        out_specs=[vec, vec, pl.BlockSpec((1,) + S5_BP_SHAPE, lambda l: (l, 0, 0, 0, 0)),
                   pl.BlockSpec((1,) + S5_W_SHAPE, lambda l: (l, 0, 0, 0, 0))],
        out_shape=[jax.ShapeDtypeStruct((DEPTH, g, 1, p), F32)] * 2
        + [jax.ShapeDtypeStruct((DEPTH,) + S5_BP_SHAPE, BF16), jax.ShapeDtypeStruct((DEPTH,) + S5_W_SHAPE, BF16)],
        compiler_params=_params("arbitrary"),
        name="s5_discretise",
    )(lr, li, ls, brt, bit, c_re, c_im)


def _memkv_kernel(mem_ref, g_ref, wk_ref, wv_ref, k_ref, v_ref):
    m = _rms(mem_ref[...], g_ref[0]).astype(BF16)
    k_ref[0] = _dot(m, wk_ref[0]).astype(BF16)
    v_ref[0] = _dot(m, wv_ref[0]).astype(BF16)


def _mem_kv(mem2d, norm_mem, w_ck, w_cv):
    rows = mem2d.shape[0]
    wspec = pl.BlockSpec((1, D_MODEL, D_MODEL), lambda l: (l, 0, 0))
    ospec = pl.BlockSpec((1, rows, D_MODEL), lambda l: (l, 0, 0))
    return pl.pallas_call(
        _memkv_kernel,
        grid=(DEPTH,),
        in_specs=[pl.BlockSpec((rows, D_MODEL), lambda l: (0, 0)),
                  pl.BlockSpec((1, 1, D_MODEL), lambda l: (l, 0, 0)), wspec, wspec],
        out_specs=[ospec, ospec],
        out_shape=[jax.ShapeDtypeStruct((DEPTH, rows, D_MODEL), BF16)] * 2,
        compiler_params=_params("arbitrary"),
        name="mem_kv",
    )(mem2d, norm_mem.reshape(DEPTH, 1, D_MODEL), w_ck, w_cv)


def _in_proj_kernel(x_ref, g_ref, w_ref, cs_ref, u_ref, ub_ref, qkv_ref, gate_ref, us_ref):
    h = _rms(x_ref[...], g_ref[...]).astype(BF16)
    cos, sin = _unpack_rope(cs_ref)
    u = _dot(h, w_ref[:, 0:S5_WIDTH])
    for c in range(S5_KT):
        us_ref[c] = u[:, c * LANES:(c + 1) * LANES]
    hc = S5_HG * S5_GROUP_CH
    for c in range(S5_KT):
        pieces = [us_ref[c, pl.ds(i, TM // S5_R, stride=S5_R), :] for i in range(S5_R)]
        for i in range(S5_R):
            u_ref[:, i * S5_WIDTH + c * LANES:i * S5_WIDTH + (c + 1) * LANES] = pieces[i]
        for half in range(S5_KT_HALVES):
            cols = jnp.concatenate([p[:, half * hc:(half + 1) * hc] for p in pieces], axis=1)
            base = (c * S5_KT_HALVES + half) * S5_R * hc
            ub_ref[:, base:base + S5_R * hc] = cols.astype(BF16)
    for dst, off, scale in ((0, S5_WIDTH, None), (RET_WIDTH, S5_WIDTH + RET_WIDTH, RET_DK ** -0.5)):
        t = _dot(h, w_ref[:, off:off + RET_WIDTH])
        for hd in range(RET_HEADS):
            th = t[:, hd * RET_DK:(hd + 1) * RET_DK]
            r = th * cos + pltpu.roll(th, RET_DK // 2, axis=1) * sin
            if scale is not None:
                r = r * scale
            qkv_ref[:, dst + hd * RET_DK:dst + (hd + 1) * RET_DK] = r.astype(BF16)
    off = S5_WIDTH + 2 * RET_WIDTH
    qkv_ref[:, 2 * RET_WIDTH:] = _dot(h, w_ref[:, off:off + RET_WIDTH]).astype(BF16)
    gate_ref[...] = _dot(h, w_ref[:, off + RET_WIDTH:off + 2 * RET_WIDTH])


def _blocked_spec(tiles_per_batch):
    return pl.BlockSpec((None, TM // S5_R, S5_R * S5_WIDTH),
                        lambda i: (i // tiles_per_batch, i % tiles_per_batch, 0))


def _in_proj(l, x2d, g, w_bf16, rope, nb, seq):
    n = x2d.shape[0]
    row = lambda w: pl.BlockSpec((TM, w), lambda i: (i, 0))
    return pl.pallas_call(
        _in_proj_kernel,
        grid=(n // TM,),
        in_specs=[row(D_MODEL), _layer_spec(l, (1, D_MODEL)), _layer_spec(l, (D_MODEL, IN_WIDTH)),
                  pl.BlockSpec((TM // 2, 2 * RET_DK), lambda i: (i, 0))],
        out_specs=[_blocked_spec(seq // TM), _blocked_spec(seq // TM), row(3 * RET_WIDTH), row(RET_WIDTH)],
        out_shape=[jax.ShapeDtypeStruct((nb, seq // S5_R, S5_R * S5_WIDTH), F32),
                   jax.ShapeDtypeStruct((nb, seq // S5_R, S5_R * S5_WIDTH), BF16),
                   jax.ShapeDtypeStruct((n, 3 * RET_WIDTH), BF16),
                   jax.ShapeDtypeStruct((n, RET_WIDTH), F32)],
        scratch_shapes=[pltpu.VMEM((S5_KT, TM, LANES), F32)],
        compiler_params=_params("arbitrary"),
        name="in_proj",
    )(x2d, g, w_bf16, rope)


def _s5_kernel(nb, u_ref, ub_ref, bp_ref, ar_ref, ai_ref, w_ref, d_ref, wglu_ref, bglu_ref, gout_ref,
               o_ref, st_ref, y_ref, *xs_refs):
    nslab = S5_SLAB // LANES
    rows = nb * S5_TK

    @pl.when(pl.program_id(0) == 0)
    def _():
        st_ref[...] = jnp.zeros_like(st_ref)

    pack = SUBLANES // nb
    npk = nslab // pack
    hc, hs = S5_HG * S5_GROUP_CH, S5_HG * S5_STATE
    half_slabs = hs // LANES

    def slab_block(s, b):
        return s // pack, ((s % pack) * nb + b) * S5_PITCH

    def half_state_slabs(half):
        re = [half * half_slabs + j for j in range(half_slabs)]
        return re + [nslab + s for s in re]

    def block_inputs(kt, half):
        base = (kt * S5_KT_HALVES + half) * S5_R * hc
        return ub_ref[:, :, base:base + S5_R * hc].reshape(rows, S5_R * hc)

    def state_input(kt, half):
        xb = _dot(block_inputs(kt, half), bp_ref[kt, half])
        for b in range(nb):
            for jj, s in enumerate(half_state_slabs(half)):
                p, r0 = slab_block(s, b)
                xs_refs[kt][p, r0:r0 + S5_TK, :] = xb[b * S5_TK:(b + 1) * S5_TK, jj * LANES:(jj + 1) * LANES]

    def readout(kt):
        def slab(s, b):
            p, r0 = slab_block(s, b)
            return xs_refs[kt][p, r0:r0 + S5_TK, :]

        yk = []
        for half in range(S5_KT_HALVES):
            s_prev = jnp.concatenate(
                [jnp.concatenate([slab(s, b) for s in half_state_slabs(half)], axis=1) for b in range(nb)],
                axis=0).astype(BF16)
            yk.append(_dot(jnp.concatenate([s_prev, block_inputs(kt, half)], axis=1), w_ref[kt, half]))
        for j in range(S5_R):
            at = slice(j * S5_WIDTH + kt * LANES, j * S5_WIDTH + (kt + 1) * LANES)
            y = (jnp.concatenate([h[:, j * hc:(j + 1) * hc] for h in yk], axis=1).reshape(nb, S5_TK, LANES)
                 + d_ref[:, kt * LANES:(kt + 1) * LANES] * u_ref[:, :, at])
            y_ref[:, :, at] = 0.5 * y * (1.0 + lax.erf(y * math.sqrt(0.5)))

    def recurrence(kt):
        def packed(ref, q):
            return jnp.concatenate(
                [jnp.broadcast_to(ref[kt, :, s * LANES:(s + 1) * LANES], (nb, LANES))
                 for s in range(q * pack, (q + 1) * pack)], axis=0)

        a_r = [packed(ar_ref, q) for q in range(npk)]
        a_i = [packed(ai_ref, q) for q in range(npk)]
        s_r = [st_ref[kt, q] for q in range(npk)]
        s_i = [st_ref[kt, npk + q] for q in range(npk)]
        for t in range(S5_TK):
            at = pl.ds(t, pack * nb, stride=S5_PITCH)
            for q in range(npk):
                x_r = xs_refs[kt][q, at, :]
                x_i = xs_refs[kt][npk + q, at, :]
                xs_refs[kt][q, at, :] = s_r[q]
                xs_refs[kt][npk + q, at, :] = s_i[q]
                s_r[q], s_i[q] = (a_r[q] * s_r[q] - a_i[q] * s_i[q] + x_r,
                                  a_r[q] * s_i[q] + a_i[q] * s_r[q] + x_i)
        for q in range(npk):
            st_ref[kt, q] = s_r[q]
            st_ref[kt, npk + q] = s_i[q]

    for kt in range(S5_KT):
        for half in range(S5_KT_HALVES):
            state_input(kt, half)
    for kt in range(S5_KT):
        recurrence(kt)
        readout(kt)

    for i in range(S5_R):
        cols = slice(i * S5_WIDTH, (i + 1) * S5_WIDTH)
        z = y_ref[:, :, cols].reshape(rows, S5_WIDTH)
        z = z * jax.nn.sigmoid(_dot(z.astype(BF16), wglu_ref[...]) + bglu_ref[...])
        o_ref[:, :, cols] = _rms(z, gout_ref[...]).astype(BF16).reshape(nb, S5_TK, S5_WIDTH)


def _s5_mixer(l, u3, ub3, bp, a_r, a_i, w, d, wglu, bglu, gout):
    nb, krows, width = u3.shape
    assert SUBLANES % nb == 0
    nbuf = 2 * S5_SLAB // LANES * nb // SUBLANES
    tok = pl.BlockSpec((nb, S5_TK, width), lambda c: (0, c, 0))
    return pl.pallas_call(
        functools.partial(_s5_kernel, nb),
        grid=(krows // S5_TK,),
        in_specs=[tok, tok,
                  _layer_spec(l, S5_BP_SHAPE),
                  _layer_spec(l, (S5_KT, 1, S5_SLAB)), _layer_spec(l, (S5_KT, 1, S5_SLAB)),
                  _layer_spec(l, S5_W_SHAPE),
                  _layer_spec(l, (1, S5_WIDTH)), _layer_spec(l, (S5_WIDTH, S5_WIDTH)),
                  _layer_spec(l, (1, S5_WIDTH)), _layer_spec(l, (1, S5_WIDTH))],
        out_specs=tok,
        out_shape=jax.ShapeDtypeStruct((nb, krows, width), BF16),
        scratch_shapes=[pltpu.VMEM((S5_KT, nbuf, SUBLANES, LANES), F32),
                        pltpu.VMEM((nb, S5_TK, width), F32)]
        + [pltpu.VMEM((nbuf, SUBLANES * S5_PITCH, LANES), F32)] * S5_KT,
        compiler_params=_params("arbitrary"),
        name="s5_mixer",
    )(u3, ub3, bp, a_r, a_i, w, d, wglu, bglu, gout)


def _ret_kernel(qkv_ref, g_ref, lg_ref, gain_ref, o_ref, st_ref, dm_ref, zeta_ref, xi_ref):
    @pl.when((pl.program_id(0) == 0) & (pl.program_id(1) == 0))
    def _():
        row = lax.broadcasted_iota(jnp.int32, (RET_CHUNK, RET_CHUNK), 0).astype(F32)
        col = lax.broadcasted_iota(jnp.int32, (RET_CHUNK, RET_CHUNK), 1).astype(F32)
        diff = row - col
        idx = lax.broadcasted_iota(jnp.int32, (RET_CHUNK, RET_DK), 0).astype(F32)
        for hd in range(RET_HEADS):
            lg = lg_ref[hd]
            lg_wide = jnp.concatenate([lg] * (RET_CHUNK // RET_DK), axis=1)
            dm_ref[hd] = jnp.where(diff >= 0, jnp.exp(jnp.maximum(diff, 0.0) * lg_wide), 0.0)
            zeta_ref[hd] = jnp.exp((RET_CHUNK - 1.0 - idx) * lg)
            xi_ref[hd] = jnp.exp((idx + 1.0) * lg)

    @pl.when(pl.program_id(1) == 0)
    def _():
        st_ref[...] = jnp.zeros_like(st_ref)

    for c in range(RET_TR // RET_CHUNK):
        rows = slice(c * RET_CHUNK, (c + 1) * RET_CHUNK)
        for hd in range(RET_HEADS):
            lanes = slice(hd * RET_DK, (hd + 1) * RET_DK)
            qh, kh, vh = (qkv_ref[rows, part * RET_WIDTH + hd * RET_DK:part * RET_WIDTH + (hd + 1) * RET_DK]
                          for part in range(3))
            s = lax.dot_general(qh, kh, (((1,), (1,)), ((), ())), preferred_element_type=F32)
            inner = _dot((s * dm_ref[hd]).astype(BF16), vh)
            state = st_ref[hd]
            cross = _dot(qh, state.astype(BF16)) * xi_ref[hd]
            kz = (kh.astype(F32) * zeta_ref[hd]).astype(BF16)
            kv = lax.dot_general(kz, vh, (((0,), (0,)), ((), ())), preferred_element_type=F32)
            st_ref[hd] = jnp.exp(RET_CHUNK * lg_ref[hd]) * state + kv
            o = inner + cross
            mu = jnp.mean(o, axis=-1, keepdims=True)
            oc = o - mu
            var = jnp.mean(oc * oc, axis=-1, keepdims=True)
            o = oc * lax.rsqrt(var + EPS) * gain_ref[hd]
            o_ref[rows, lanes] = (o * jax.nn.silu(g_ref[rows, lanes])).astype(BF16)


def _retention(l, qkv, gate, lg, gain, nb, seq):
    steps = seq // RET_TR
    tok = pl.BlockSpec((RET_TR, RET_WIDTH), lambda b, c: (b * steps + c, 0))
    return pl.pallas_call(
        _ret_kernel,
        grid=(nb, steps),
        in_specs=[pl.BlockSpec((RET_TR, 3 * RET_WIDTH), lambda b, c: (b * steps + c, 0)), tok,
                  _const_spec((RET_HEADS, 1, RET_DK)), _layer_spec(l, (RET_HEADS, 1, RET_DK))],
        out_specs=tok,
        out_shape=jax.ShapeDtypeStruct((nb * seq, RET_WIDTH), BF16),
        scratch_shapes=[pltpu.VMEM((RET_HEADS, RET_DK, RET_DK), F32),
                        pltpu.VMEM((RET_HEADS, RET_CHUNK, RET_CHUNK), F32),
                        pltpu.VMEM((RET_HEADS, RET_CHUNK, RET_DK), F32),
                        pltpu.VMEM((RET_HEADS, RET_CHUNK, RET_DK), F32)],
        compiler_params=_params("arbitrary", "arbitrary"),
        name="retention",
    )(qkv, gate, lg, gain)


def _mix_out_kernel(x_ref, ys_ref, yr_ref, wo_ref, g_ref, wq_ref, k_ref, v_ref, wco_ref, o_ref, ps_ref):
    tk = TM // S5_R
    ys = jnp.concatenate([ys_ref[:, i * S5_WIDTH:(i + 1) * S5_WIDTH] for i in range(S5_R)], axis=0)
    ps = _dot(ys, wo_ref[0:S5_WIDTH, :])
    for i in range(S5_R):
        for c in range(D_MODEL // LANES):
            ps_ref[c, pl.ds(i, tk, stride=S5_R), :] = ps[i * tk:(i + 1) * tk, c * LANES:(c + 1) * LANES]
    ssm = jnp.concatenate([ps_ref[c] for c in range(D_MODEL // LANES)], axis=1)
    x = x_ref[...] + ssm + _dot(yr_ref[...], wo_ref[S5_WIDTH:, :])
    h = _rms(x, g_ref[...]).astype(BF16)
    q = _dot(h, wq_ref[...]).astype(BF16)
    outs = []
    for hd in range(X_HEADS):
        lanes = slice(hd * X_HEAD_DIM, (hd + 1) * X_HEAD_DIM)
        s = lax.dot_general(q[:, lanes], k_ref[:, lanes], (((1,), (1,)), ((), ())),
                            preferred_element_type=F32)
        e = jnp.exp(s - jnp.max(s, axis=-1, keepdims=True))
        p = e * (1.0 / jnp.sum(e, axis=-1, keepdims=True))
        outs.append(_dot(p.astype(BF16), v_ref[:, lanes]).astype(BF16))
    o = jnp.concatenate(outs, axis=1)
    o_ref[...] = x + _dot(o, wco_ref[...])


def _mix_out(l, x2d, y_ssm, y_ret, w_out, g, wq, k_mem, v_mem, wco, tiles_per_batch):
    n = x2d.shape[0]
    row = lambda w: pl.BlockSpec((TM, w), lambda i: (i, 0))
    mem = pl.BlockSpec((None, MEM_LEN, D_MODEL), lambda i: (l, i // tiles_per_batch, 0))
    sq = _layer_spec(l, (D_MODEL, D_MODEL))
    return pl.pallas_call(
        _mix_out_kernel,
        grid=(n // TM,),
        in_specs=[row(D_MODEL), _blocked_spec(tiles_per_batch),
                  row(RET_WIDTH), sq, _layer_spec(l, (1, D_MODEL)), sq, mem, mem, sq],
        out_specs=row(D_MODEL),
        out_shape=jax.ShapeDtypeStruct((n, D_MODEL), F32),
        scratch_shapes=[pltpu.VMEM((D_MODEL // LANES, TM, LANES), F32)],
        compiler_params=_params("arbitrary"),
        name="mix_out_xattn",
    )(x2d, y_ssm, y_ret, w_out, g, wq, k_mem, v_mem, wco)


def _ffn_kernel(final, x_ref, g_ref, wg_ref, wu_ref, wd_ref, gf_ref, o_ref):
    x = x_ref[...]
    h = _rms(x, g_ref[...]).astype(BF16)
    acts = [(jax.nn.silu(_dot(h, wg_ref[:, lo:hi])) * _dot(h, wu_ref[:, lo:hi])).astype(BF16)
            for lo, hi in FF_SPLITS]
    acc = x + _dot(jnp.concatenate(acts, axis=1), wd_ref[...])
    if final:
        acc = _rms(acc, gf_ref[...])
    o_ref[...] = acc


def _ffn(l, x2d, g, wg, wu, wd, g_final, final):
    n = x2d.shape[0]
    row = pl.BlockSpec((TM, D_MODEL), lambda i: (i, 0))
    return pl.pallas_call(
        functools.partial(_ffn_kernel, final),
        grid=(n // TM,),
        in_specs=[row, _layer_spec(l, (1, D_MODEL)), _layer_spec(l, (D_MODEL, D_FF)),
                  _layer_spec(l, (D_MODEL, D_FF)), _layer_spec(l, (D_FF, D_MODEL)), _const_spec((1, D_MODEL))],
        out_specs=row,
        out_shape=jax.ShapeDtypeStruct((n, D_MODEL), F32),
        compiler_params=_params("arbitrary"),
        name="ffn_final" if final else "ffn",
    )(x2d, g, wg, wu, wd, g_final.reshape(1, D_MODEL))


def kernel(x, mem, positions, norm_mix, w_in, s5_lambda_re, s5_lambda_im, s5_log_step, s5_b_re, s5_b_im, s5_c_re, s5_c_im, s5_d, s5_w_glu, s5_b_glu, s5_out_norm, ret_out_norm, w_out, norm_cross, norm_mem, w_cq, w_ck, w_cv, w_co, norm_ffn, w_gate, w_up, w_down, norm_final):
    nb, seq, _ = x.shape
    n = nb * seq
    assert seq % RET_TR == 0 and seq % (S5_R * S5_TK) == 0 and seq % TM == 0
    bf = lambda a: a.astype(BF16)
    vec = lambda a: a.reshape(DEPTH, 1, a.shape[-1])

    rope = _rope_tables(positions)
    a_r, a_i, s5_bp, s5_w = _s5_discretise(
        s5_lambda_re, s5_lambda_im, s5_log_step, s5_b_re, s5_b_im, s5_c_re, s5_c_im)
    a_r = a_r.reshape(DEPTH, S5_KT, 1, S5_SLAB)
    a_i = a_i.reshape(DEPTH, S5_KT, 1, S5_SLAB)
    k_mem, v_mem = _mem_kv(mem.reshape(nb * MEM_LEN, D_MODEL), norm_mem, bf(w_ck), bf(w_cv))
    lg = jnp.log1p(-jnp.exp2(-5.0 - jnp.arange(RET_HEADS, dtype=F32)))
    lg = jnp.broadcast_to(lg[:, None, None], (RET_HEADS, 1, RET_DK))
    ret_gain = ret_out_norm.reshape(DEPTH, RET_HEADS, 1, RET_DK)
    w_in_b, w_out_b, w_cq_b, w_co_b = bf(w_in), bf(w_out), bf(w_cq * X_HEAD_DIM ** -0.5), bf(w_co)
    w_gate_b, w_up_b, w_down_b, w_glu_b = bf(w_gate), bf(w_up), bf(w_down), bf(s5_w_glu)
    g_mix, g_cross, g_ffn = vec(norm_mix), vec(norm_cross), vec(norm_ffn)
    s5_d3, s5_bglu3, s5_gout3 = vec(s5_d), vec(s5_b_glu), vec(s5_out_norm)

    x2d = x.reshape(n, D_MODEL)
    for l in range(DEPTH):
        u, ub, qkv, gate = _in_proj(l, x2d, g_mix, w_in_b, rope, nb, seq)
        y_ssm = _s5_mixer(l, u, ub, s5_bp, a_r, a_i, s5_w,
                          s5_d3, w_glu_b, s5_bglu3, s5_gout3)
        y_ret = _retention(l, qkv, gate, lg, ret_gain, nb, seq)
        x2d = _mix_out(l, x2d, y_ssm, y_ret, w_out_b, g_cross, w_cq_b,
                       k_mem, v_mem, w_co_b, seq // TM)
        x2d = _ffn(l, x2d, g_ffn, w_gate_b, w_up_b, w_down_b, norm_final, l == DEPTH - 1)
    return x2d.reshape(nb, seq, D_MODEL)
```

```python
import functools
import math

import jax
import jax.numpy as jnp
from jax import lax
from jax.experimental import pallas as pl
from jax.experimental.pallas import tpu as pltpu

D_MODEL = 1024
DEPTH = 4
MEM_LEN = 256
S5_WIDTH = 512
S5_GROUP_CH = 16
S5_GROUPS = 32
S5_STATE = 64
RET_HEADS = 4
RET_DK = 128
RET_WIDTH = 512
IN_WIDTH = S5_WIDTH + 4 * RET_WIDTH
X_HEADS = 4
X_HEAD_DIM = 256
D_FF = 2816
ROPE_BASE = 10000.0
EPS = 1e-6

LANES = 128
SUBLANES = 8
S5_KT = S5_WIDTH // LANES
S5_GPT = LANES // S5_GROUP_CH
S5_SLAB = S5_GPT * S5_STATE
MXU_DIM = 256
S5_HG = MXU_DIM // S5_STATE
S5_KT_HALVES = S5_GPT // S5_HG
S5_R = 4
S5_BP_SHAPE = (S5_KT, S5_KT_HALVES, S5_R * S5_HG * S5_GROUP_CH, 2 * S5_HG * S5_STATE)
S5_W_SHAPE = (S5_KT, S5_KT_HALVES, (2 * S5_STATE + S5_R * S5_GROUP_CH) * S5_HG,
              S5_R * S5_HG * S5_GROUP_CH)
S5_TK = 128
S5_PITCH = S5_TK + 8
RET_CHUNK = 256
RET_TR = 2048
TM = 1024
FF_SPLITS = ((0, 1536), (1536, D_FF))

BF16 = jnp.bfloat16
F32 = jnp.float32
VMEM_LIMIT = 48 * 1024 * 1024
SINGLE_BUFFER_MIN_ELEMS = 64 * 1024


def _dot(a, b):
    return jnp.dot(a, b, preferred_element_type=F32)


def _rms(x, g):
    ms = jnp.mean(x * x, axis=-1, keepdims=True)
    return x * lax.rsqrt(ms + EPS) * g


def _resident(shape):
    return dict(pipeline_mode=pl.Buffered(1)) if math.prod(shape) >= SINGLE_BUFFER_MIN_ELEMS else {}


def _const_spec(shape):
    nd = len(shape)
    return pl.BlockSpec(shape, lambda *_: (0,) * nd, **_resident(shape))


def _layer_spec(l, shape):
    nd = len(shape)
    return pl.BlockSpec((None,) + tuple(shape), lambda *_: (l,) + (0,) * nd, **_resident(shape))


def _params(*sem):
    return pltpu.CompilerParams(dimension_semantics=sem, vmem_limit_bytes=VMEM_LIMIT)


def _rope_kernel(pos_ref, invf_ref, cs_ref):
    low = lax.broadcasted_iota(jnp.int32, (cs_ref.shape[0], RET_DK), 1) < RET_DK // 2
    ang = jnp.where(low, pos_ref[:, 0:1], pos_ref[:, 1:2]) * invf_ref[...]
    cs_ref[:, 0:RET_DK] = jnp.cos(ang)
    cs_ref[:, RET_DK:] = jnp.sin(ang)


def _rope_tables(positions):
    n = positions.size
    half = RET_DK // 2
    inv_freq = 1.0 / (ROPE_BASE ** (jnp.arange(half, dtype=F32) / half))
    invf2 = jnp.concatenate([inv_freq, inv_freq]).reshape(1, RET_DK)
    pos = positions.astype(F32).reshape(n // TM, 2, TM // 2)
    pos = jnp.swapaxes(pos, 1, 2).reshape(n // 2, 2)
    t = 1024
    return pl.pallas_call(
        _rope_kernel,
        grid=(n // 2 // t,),
        in_specs=[pl.BlockSpec((t, 2), lambda i: (i, 0)),
                  pl.BlockSpec((1, RET_DK), lambda i: (0, 0))],
        out_specs=pl.BlockSpec((t, 2 * RET_DK), lambda i: (i, 0)),
        out_shape=jax.ShapeDtypeStruct((n // 2, 2 * RET_DK), F32),
        compiler_params=_params("arbitrary"),
        name="rope_tables",
    )(pos, invf2)


def _unpack_rope(cs_ref):
    half = RET_DK // 2
    c, s = cs_ref[:, 0:RET_DK], cs_ref[:, RET_DK:]
    low = lax.broadcasted_iota(jnp.int32, c.shape, 1) < half
    c_sw, s_sw = pltpu.roll(c, half, axis=1), pltpu.roll(s, half, axis=1)
    cos = jnp.concatenate([jnp.where(low, c, c_sw), jnp.where(low, c_sw, c)], axis=0)
    sin = jnp.concatenate([jnp.where(low, -s, s_sw), jnp.where(low, -s_sw, s)], axis=0)
    return cos, sin


def _tile_lanes(x, reps):
    while reps > 1:
        x = jnp.concatenate([x, x], axis=1)
        reps //= 2
    return x


def _block_diag(blocks, kt):
    _, r, c = blocks.shape
    x = _tile_lanes(blocks[kt * S5_GPT:(kt + 1) * S5_GPT].reshape(S5_GPT * r, c), S5_GPT)
    row_group = lax.broadcasted_iota(jnp.int32, x.shape, 0) // r
    col_group = lax.broadcasted_iota(jnp.int32, x.shape, 1) // c
    return jnp.where(row_group == col_group, x, 0.0)


def _s5_disc_kernel(lr_ref, li_ref, ls_ref, brt_ref, bit_ref, cr_ref, ci_ref, ar_ref, ai_ref, bp_ref, w_ref):
    lr = lr_ref[0]
    li = li_ref[0]
    step = jnp.exp(ls_ref[0])
    mag = jnp.exp(lr * step)
    a_r = mag * jnp.cos(li * step)
    a_i = mag * jnp.sin(li * step)
    den = lr * lr + li * li
    f_r = ((a_r - 1.0) * lr + a_i * li) / den
    f_i = (a_i * lr - (a_r - 1.0) * li) / den
    br = brt_ref[0]
    bi = bit_ref[0]
    bb_r = f_r * br - f_i * bi
    bb_i = f_r * bi + f_i * br
    cr = cr_ref[0]
    ci = ci_ref[0]
    contract_p = (((2,), (2,)), ((0,), (0,)))
    hi = dict(precision=lax.Precision.HIGHEST, preferred_element_type=F32)
    hc, hs = S5_HG * S5_GROUP_CH, S5_HG * S5_STATE
    pw_r = jnp.ones_like(a_r)
    pw_i = jnp.zeros_like(a_i)
    ro_r, ro_i, lags = [], [], []
    for m in range(S5_R):
        abb_r = pw_r * bb_r - pw_i * bb_i
        abb_i = pw_r * bb_i + pw_i * bb_r
        lag = lax.dot_general(abb_r, cr, contract_p, **hi) - lax.dot_general(abb_i, ci, contract_p, **hi)
        pw_r, pw_i = pw_r * a_r - pw_i * a_i, pw_r * a_i + pw_i * a_r
        cp_r = cr * pw_r - ci * pw_i
        cp_i = cr * pw_i + ci * pw_r
        i = S5_R - 1 - m
        for kt in range(S5_KT):
            bd_r = _block_diag(abb_r, kt)
            bd_i = _block_diag(abb_i, kt)
            for half in range(S5_KT_HALVES):
                rs, cs = slice(half * hc, (half + 1) * hc), slice(half * hs, (half + 1) * hs)
                bp_ref[0, kt, half, i * hc:(i + 1) * hc, :] = jnp.concatenate(
                    [bd_r[rs, cs], bd_i[rs, cs]], axis=1).astype(BF16)
        ro_r.append([_block_diag(cp_r, kt).T for kt in range(S5_KT)])
        ro_i.append([(-_block_diag(cp_i, kt)).T for kt in range(S5_KT)])
        lags.append([_block_diag(lag, kt) for kt in range(S5_KT)])
    zero = jnp.zeros((hc, hc), F32)
    for kt in range(S5_KT):
        for half in range(S5_KT_HALVES):
            rs, cs = slice(half * hs, (half + 1) * hs), slice(half * hc, (half + 1) * hc)
            w_ref[0, kt, half, 0:hs, :] = jnp.concatenate(
                [ro_r[j][kt][rs, cs] for j in range(S5_R)], axis=1).astype(BF16)
            w_ref[0, kt, half, hs:2 * hs, :] = jnp.concatenate(
                [ro_i[j][kt][rs, cs] for j in range(S5_R)], axis=1).astype(BF16)
            for i in range(S5_R):
                w_ref[0, kt, half, 2 * hs + i * hc:2 * hs + (i + 1) * hc, :] = jnp.concatenate(
                    [lags[j - i][kt][cs, cs] if j >= i else zero for j in range(S5_R)], axis=1).astype(BF16)
    ar_ref[0] = pw_r
    ai_ref[0] = pw_i


def _s5_discretise(lam_re, lam_im, log_step, b_re, b_im, c_re, c_im):
    g, p, h = S5_GROUPS, S5_STATE, S5_GROUP_CH
    lr = lam_re.reshape(DEPTH, g, 1, p)
    li = lam_im.reshape(DEPTH, g, 1, p)
    ls = log_step.reshape(DEPTH, g, 1, 1)
    brt = jnp.swapaxes(b_re, -1, -2)
    bit = jnp.swapaxes(b_im, -1, -2)
    vec = pl.BlockSpec((1, g, 1, p), lambda l: (l, 0, 0, 0))
    mat = pl.BlockSpec((1, g, h, p), lambda l: (l, 0, 0, 0))
    return pl.pallas_call(
        _s5_disc_kernel,
        grid=(DEPTH,),
        in_specs=[vec, vec, pl.BlockSpec((1, g, 1, 1), lambda l: (l, 0, 0, 0)), mat, mat, mat, mat],
        out_specs=[vec, vec, pl.BlockSpec((1,) + S5_BP_SHAPE, lambda l: (l, 0, 0, 0, 0)),
                   pl.BlockSpec((1,) + S5_W_SHAPE, lambda l: (l, 0, 0, 0, 0))],
        out_shape=[jax.ShapeDtypeStruct((DEPTH, g, 1, p), F32)] * 2
        + [jax.ShapeDtypeStruct((DEPTH,) + S5_BP_SHAPE, BF16), jax.ShapeDtypeStruct((DEPTH,) + S5_W_SHAPE, BF16)],
        compiler_params=_params("arbitrary"),
        name="s5_discretise",
    )(lr, li, ls, brt, bit, c_re, c_im)


def _memkv_kernel(mem_ref, g_ref, wk_ref, wv_ref, k_ref, v_ref):
    m = _rms(mem_ref[...], g_ref[0]).astype(BF16)
    k_ref[0] = _dot(m, wk_ref[0]).astype(BF16)
    v_ref[0] = _dot(m, wv_ref[0]).astype(BF16)


def _mem_kv(mem2d, norm_mem, w_ck, w_cv):
    rows = mem2d.shape[0]
    wspec = pl.BlockSpec((1, D_MODEL, D_MODEL), lambda l: (l, 0, 0))
    ospec = pl.BlockSpec((1, rows, D_MODEL), lambda l: (l, 0, 0))
    return pl.pallas_call(
        _memkv_kernel,
        grid=(DEPTH,),
        in_specs=[pl.BlockSpec((rows, D_MODEL), lambda l: (0, 0)),
                  pl.BlockSpec((1, 1, D_MODEL), lambda l: (l, 0, 0)), wspec, wspec],
        out_specs=[ospec, ospec],
        out_shape=[jax.ShapeDtypeStruct((DEPTH, rows, D_MODEL), BF16)] * 2,
        compiler_params=_params("arbitrary"),
        name="mem_kv",
    )(mem2d, norm_mem.reshape(DEPTH, 1, D_MODEL), w_ck, w_cv)


def _in_proj_kernel(x_ref, g_ref, w_ref, cs_ref, u_ref, ub_ref, qkv_ref, gate_ref, us_ref):
    h = _rms(x_ref[...], g_ref[...]).astype(BF16)
    cos, sin = _unpack_rope(cs_ref)
    u = _dot(h, w_ref[:, 0:S5_WIDTH])
    for c in range(S5_KT):
        us_ref[c] = u[:, c * LANES:(c + 1) * LANES]
    hc = S5_HG * S5_GROUP_CH
    for c in range(S5_KT):
        pieces = [us_ref[c, pl.ds(i, TM // S5_R, stride=S5_R), :] for i in range(S5_R)]
        for i in range(S5_R):
            u_ref[:, i * S5_WIDTH + c * LANES:i * S5_WIDTH + (c + 1) * LANES] = pieces[i]
        for half in range(S5_KT_HALVES):
            cols = jnp.concatenate([p[:, half * hc:(half + 1) * hc] for p in pieces], axis=1)
            base = (c * S5_KT_HALVES + half) * S5_R * hc
            ub_ref[:, base:base + S5_R * hc] = cols.astype(BF16)
    for dst, off, scale in ((0, S5_WIDTH, None), (RET_WIDTH, S5_WIDTH + RET_WIDTH, RET_DK ** -0.5)):
        t = _dot(h, w_ref[:, off:off + RET_WIDTH])
        for hd in range(RET_HEADS):
            th = t[:, hd * RET_DK:(hd + 1) * RET_DK]
            r = th * cos + pltpu.roll(th, RET_DK // 2, axis=1) * sin
            if scale is not None:
                r = r * scale
            qkv_ref[:, dst + hd * RET_DK:dst + (hd + 1) * RET_DK] = r.astype(BF16)
    off = S5_WIDTH + 2 * RET_WIDTH
    qkv_ref[:, 2 * RET_WIDTH:] = _dot(h, w_ref[:, off:off + RET_WIDTH]).astype(BF16)
    gate_ref[...] = _dot(h, w_ref[:, off + RET_WIDTH:off + 2 * RET_WIDTH])


def _blocked_spec(tiles_per_batch):
    return pl.BlockSpec((None, TM // S5_R, S5_R * S5_WIDTH),
                        lambda i: (i // tiles_per_batch, i % tiles_per_batch, 0))


def _in_proj(l, x2d, g, w_bf16, rope, nb, seq):
    n = x2d.shape[0]
    row = lambda w: pl.BlockSpec((TM, w), lambda i: (i, 0))
    return pl.pallas_call(
        _in_proj_kernel,
        grid=(n // TM,),
        in_specs=[row(D_MODEL), _layer_spec(l, (1, D_MODEL)), _layer_spec(l, (D_MODEL, IN_WIDTH)),
                  pl.BlockSpec((TM // 2, 2 * RET_DK), lambda i: (i, 0))],
        out_specs=[_blocked_spec(seq // TM), _blocked_spec(seq // TM), row(3 * RET_WIDTH), row(RET_WIDTH)],
        out_shape=[jax.ShapeDtypeStruct((nb, seq // S5_R, S5_R * S5_WIDTH), F32),
                   jax.ShapeDtypeStruct((nb, seq // S5_R, S5_R * S5_WIDTH), BF16),
                   jax.ShapeDtypeStruct((n, 3 * RET_WIDTH), BF16),
                   jax.ShapeDtypeStruct((n, RET_WIDTH), F32)],
        scratch_shapes=[pltpu.VMEM((S5_KT, TM, LANES), F32)],
        compiler_params=_params("arbitrary"),
        name="in_proj",
    )(x2d, g, w_bf16, rope)


def _s5_kernel(nb, u_ref, ub_ref, bp_ref, ar_ref, ai_ref, w_ref, d_ref, wglu_ref, bglu_ref, gout_ref,
               o_ref, st_ref, y_ref, *xs_refs):
    nslab = S5_SLAB // LANES
    rows = nb * S5_TK

    @pl.when(pl.program_id(0) == 0)
    def _():
        st_ref[...] = jnp.zeros_like(st_ref)

    pack = SUBLANES // nb
    npk = nslab // pack
    hc, hs = S5_HG * S5_GROUP_CH, S5_HG * S5_STATE
    half_slabs = hs // LANES

    def slab_block(s, b):
        return s // pack, ((s % pack) * nb + b) * S5_PITCH

    def half_state_slabs(half):
        re = [half * half_slabs + j for j in range(half_slabs)]
        return re + [nslab + s for s in re]

    def block_inputs(kt, half):
        base = (kt * S5_KT_HALVES + half) * S5_R * hc
        return ub_ref[:, :, base:base + S5_R * hc].reshape(rows, S5_R * hc)

    def state_input(kt, half):
        xb = _dot(block_inputs(kt, half), bp_ref[kt, half])
        for b in range(nb):
            for jj, s in enumerate(half_state_slabs(half)):
                p, r0 = slab_block(s, b)
                xs_refs[kt][p, r0:r0 + S5_TK, :] = xb[b * S5_TK:(b + 1) * S5_TK, jj * LANES:(jj + 1) * LANES]

    def readout(kt):
        def slab(s, b):
            p, r0 = slab_block(s, b)
            return xs_refs[kt][p, r0:r0 + S5_TK, :]

        yk = []
        for half in range(S5_KT_HALVES):
            s_prev = jnp.concatenate(
                [jnp.concatenate([slab(s, b) for s in half_state_slabs(half)], axis=1) for b in range(nb)],
                axis=0).astype(BF16)
            yk.append(_dot(jnp.concatenate([s_prev, block_inputs(kt, half)], axis=1), w_ref[kt, half]))
        for j in range(S5_R):
            at = slice(j * S5_WIDTH + kt * LANES, j * S5_WIDTH + (kt + 1) * LANES)
            y = (jnp.concatenate([h[:, j * hc:(j + 1) * hc] for h in yk], axis=1).reshape(nb, S5_TK, LANES)
                 + d_ref[:, kt * LANES:(kt + 1) * LANES] * u_ref[:, :, at])
            y_ref[:, :, at] = 0.5 * y * (1.0 + lax.erf(y * math.sqrt(0.5)))

    def recurrence(kt):
        def packed(ref, q):
            return jnp.concatenate(
                [jnp.broadcast_to(ref[kt, :, s * LANES:(s + 1) * LANES], (nb, LANES))
                 for s in range(q * pack, (q + 1) * pack)], axis=0)

        a_r = [packed(ar_ref, q) for q in range(npk)]
        a_i = [packed(ai_ref, q) for q in range(npk)]
        s_r = [st_ref[kt, q] for q in range(npk)]
        s_i = [st_ref[kt, npk + q] for q in range(npk)]
        for t in range(S5_TK):
            at = pl.ds(t, pack * nb, stride=S5_PITCH)
            for q in range(npk):
                x_r = xs_refs[kt][q, at, :]
                x_i = xs_refs[kt][npk + q, at, :]
                xs_refs[kt][q, at, :] = s_r[q]
                xs_refs[kt][npk + q, at, :] = s_i[q]
                s_r[q], s_i[q] = (a_r[q] * s_r[q] - a_i[q] * s_i[q] + x_r,
                                  a_r[q] * s_i[q] + a_i[q] * s_r[q] + x_i)
        for q in range(npk):
            st_ref[kt, q] = s_r[q]
            st_ref[kt, npk + q] = s_i[q]

    for kt in range(S5_KT):
        for half in range(S5_KT_HALVES):
            state_input(kt, half)
    for kt in range(S5_KT):
        recurrence(kt)
        readout(kt)

    for i in range(S5_R):
        cols = slice(i * S5_WIDTH, (i + 1) * S5_WIDTH)
        z = y_ref[:, :, cols].reshape(rows, S5_WIDTH)
        z = z * jax.nn.sigmoid(_dot(z.astype(BF16), wglu_ref[...]) + bglu_ref[...])
        o_ref[:, :, cols] = _rms(z, gout_ref[...]).astype(BF16).reshape(nb, S5_TK, S5_WIDTH)


def _s5_mixer(l, u3, ub3, bp, a_r, a_i, w, d, wglu, bglu, gout):
    nb, krows, width = u3.shape
    assert SUBLANES % nb == 0
    nbuf = 2 * S5_SLAB // LANES * nb // SUBLANES
    tok = pl.BlockSpec((nb, S5_TK, width), lambda c: (0, c, 0))
    return pl.pallas_call(
        functools.partial(_s5_kernel, nb),
        grid=(krows // S5_TK,),
        in_specs=[tok, tok,
                  _layer_spec(l, S5_BP_SHAPE),
                  _layer_spec(l, (S5_KT, 1, S5_SLAB)), _layer_spec(l, (S5_KT, 1, S5_SLAB)),
                  _layer_spec(l, S5_W_SHAPE),
                  _layer_spec(l, (1, S5_WIDTH)), _layer_spec(l, (S5_WIDTH, S5_WIDTH)),
                  _layer_spec(l, (1, S5_WIDTH)), _layer_spec(l, (1, S5_WIDTH))],
        out_specs=tok,
        out_shape=jax.ShapeDtypeStruct((nb, krows, width), BF16),
        scratch_shapes=[pltpu.VMEM((S5_KT, nbuf, SUBLANES, LANES), F32),
                        pltpu.VMEM((nb, S5_TK, width), F32)]
        + [pltpu.VMEM((nbuf, SUBLANES * S5_PITCH, LANES), F32)] * S5_KT,
        compiler_params=_params("arbitrary"),
        name="s5_mixer",
    )(u3, ub3, bp, a_r, a_i, w, d, wglu, bglu, gout)


def _ret_kernel(qkv_ref, g_ref, lg_ref, gain_ref, o_ref, st_ref, dm_ref, zeta_ref, xi_ref):
    @pl.when((pl.program_id(0) == 0) & (pl.program_id(1) == 0))
    def _():
        row = lax.broadcasted_iota(jnp.int32, (RET_CHUNK, RET_CHUNK), 0).astype(F32)
        col = lax.broadcasted_iota(jnp.int32, (RET_CHUNK, RET_CHUNK), 1).astype(F32)
        diff = row - col
        idx = lax.broadcasted_iota(jnp.int32, (RET_CHUNK, RET_DK), 0).astype(F32)
        for hd in range(RET_HEADS):
            lg = lg_ref[hd]
            lg_wide = jnp.concatenate([lg] * (RET_CHUNK // RET_DK), axis=1)
            dm_ref[hd] = jnp.where(diff >= 0, jnp.exp(jnp.maximum(diff, 0.0) * lg_wide), 0.0)
            zeta_ref[hd] = jnp.exp((RET_CHUNK - 1.0 - idx) * lg)
            xi_ref[hd] = jnp.exp((idx + 1.0) * lg)

    @pl.when(pl.program_id(1) == 0)
    def _():
        st_ref[...] = jnp.zeros_like(st_ref)

    for c in range(RET_TR // RET_CHUNK):
        rows = slice(c * RET_CHUNK, (c + 1) * RET_CHUNK)
        for hd in range(RET_HEADS):
            lanes = slice(hd * RET_DK, (hd + 1) * RET_DK)
            qh, kh, vh = (qkv_ref[rows, part * RET_WIDTH + hd * RET_DK:part * RET_WIDTH + (hd + 1) * RET_DK]
                          for part in range(3))
            s = lax.dot_general(qh, kh, (((1,), (1,)), ((), ())), preferred_element_type=F32)
            inner = _dot((s * dm_ref[hd]).astype(BF16), vh)
            state = st_ref[hd]
            cross = _dot(qh, state.astype(BF16)) * xi_ref[hd]
            kz = (kh.astype(F32) * zeta_ref[hd]).astype(BF16)
            kv = lax.dot_general(kz, vh, (((0,), (0,)), ((), ())), preferred_element_type=F32)
            st_ref[hd] = jnp.exp(RET_CHUNK * lg_ref[hd]) * state + kv
            o = inner + cross
            mu = jnp.mean(o, axis=-1, keepdims=True)
            oc = o - mu
            var = jnp.mean(oc * oc, axis=-1, keepdims=True)
            o = oc * lax.rsqrt(var + EPS) * gain_ref[hd]
            o_ref[rows, lanes] = (o * jax.nn.silu(g_ref[rows, lanes])).astype(BF16)


def _retention(l, qkv, gate, lg, gain, nb, seq):
    steps = seq // RET_TR
    tok = pl.BlockSpec((RET_TR, RET_WIDTH), lambda b, c: (b * steps + c, 0))
    return pl.pallas_call(
        _ret_kernel,
        grid=(nb, steps),
        in_specs=[pl.BlockSpec((RET_TR, 3 * RET_WIDTH), lambda b, c: (b * steps + c, 0)), tok,
                  _const_spec((RET_HEADS, 1, RET_DK)), _layer_spec(l, (RET_HEADS, 1, RET_DK))],
        out_specs=tok,
        out_shape=jax.ShapeDtypeStruct((nb * seq, RET_WIDTH), BF16),
        scratch_shapes=[pltpu.VMEM((RET_HEADS, RET_DK, RET_DK), F32),
                        pltpu.VMEM((RET_HEADS, RET_CHUNK, RET_CHUNK), F32),
                        pltpu.VMEM((RET_HEADS, RET_CHUNK, RET_DK), F32),
                        pltpu.VMEM((RET_HEADS, RET_CHUNK, RET_DK), F32)],
        compiler_params=_params("arbitrary", "arbitrary"),
        name="retention",
    )(qkv, gate, lg, gain)


def _mix_out_kernel(x_ref, ys_ref, yr_ref, wo_ref, g_ref, wq_ref, k_ref, v_ref, wco_ref, o_ref, ps_ref):
    tk = TM // S5_R
    ys = jnp.concatenate([ys_ref[:, i * S5_WIDTH:(i + 1) * S5_WIDTH] for i in range(S5_R)], axis=0)
    ps = _dot(ys, wo_ref[0:S5_WIDTH, :])
    for i in range(S5_R):
        for c in range(D_MODEL // LANES):
            ps_ref[c, pl.ds(i, tk, stride=S5_R), :] = ps[i * tk:(i + 1) * tk, c * LANES:(c + 1) * LANES]
    ssm = jnp.concatenate([ps_ref[c] for c in range(D_MODEL // LANES)], axis=1)
    x = x_ref[...] + ssm + _dot(yr_ref[...], wo_ref[S5_WIDTH:, :])
    h = _rms(x, g_ref[...]).astype(BF16)
    q = _dot(h, wq_ref[...]).astype(BF16)
    outs = []
    for hd in range(X_HEADS):
        lanes = slice(hd * X_HEAD_DIM, (hd + 1) * X_HEAD_DIM)
        s = lax.dot_general(q[:, lanes], k_ref[:, lanes], (((1,), (1,)), ((), ())),
                            preferred_element_type=F32)
        e = jnp.exp(s - jnp.max(s, axis=-1, keepdims=True))
        p = e * (1.0 / jnp.sum(e, axis=-1, keepdims=True))
        outs.append(_dot(p.astype(BF16), v_ref[:, lanes]).astype(BF16))
    o = jnp.concatenate(outs, axis=1)
    o_ref[...] = x + _dot(o, wco_ref[...])


def _mix_out(l, x2d, y_ssm, y_ret, w_out, g, wq, k_mem, v_mem, wco, tiles_per_batch):
    n = x2d.shape[0]
    row = lambda w: pl.BlockSpec((TM, w), lambda i: (i, 0))
    mem = pl.BlockSpec((None, MEM_LEN, D_MODEL), lambda i: (l, i // tiles_per_batch, 0))
    sq = _layer_spec(l, (D_MODEL, D_MODEL))
    return pl.pallas_call(
        _mix_out_kernel,
        grid=(n // TM,),
        in_specs=[row(D_MODEL), _blocked_spec(tiles_per_batch),
                  row(RET_WIDTH), sq, _layer_spec(l, (1, D_MODEL)), sq, mem, mem, sq],
        out_specs=row(D_MODEL),
        out_shape=jax.ShapeDtypeStruct((n, D_MODEL), F32),
        scratch_shapes=[pltpu.VMEM((D_MODEL // LANES, TM, LANES), F32)],
        compiler_params=_params("arbitrary"),
        name="mix_out_xattn",
    )(x2d, y_ssm, y_ret, w_out, g, wq, k_mem, v_mem, wco)


def _ffn_kernel(final, x_ref, g_ref, wg_ref, wu_ref, wd_ref, gf_ref, o_ref):
    for r0 in range(0, TM, TM // 2):
        rows = slice(r0, r0 + TM // 2)
        x = x_ref[rows, :]
        h = _rms(x, g_ref[...]).astype(BF16)
        acts = [(jax.nn.silu(_dot(h, wg_ref[:, lo:hi])) * _dot(h, wu_ref[:, lo:hi])).astype(BF16)
                for lo, hi in FF_SPLITS]
        acc = x + _dot(jnp.concatenate(acts, axis=1), wd_ref[...])
        if final:
            acc = _rms(acc, gf_ref[...])
        o_ref[rows, :] = acc


def _ffn(l, x2d, g, wg, wu, wd, g_final, final):
    n = x2d.shape[0]
    row = pl.BlockSpec((TM, D_MODEL), lambda i: (i, 0))
    return pl.pallas_call(
        functools.partial(_ffn_kernel, final),
        grid=(n // TM,),
        in_specs=[row, _layer_spec(l, (1, D_MODEL)), _layer_spec(l, (D_MODEL, D_FF)),
                  _layer_spec(l, (D_MODEL, D_FF)), _layer_spec(l, (D_FF, D_MODEL)), _const_spec((1, D_MODEL))],
        out_specs=row,
        out_shape=jax.ShapeDtypeStruct((n, D_MODEL), F32),
        compiler_params=_params("arbitrary"),
        name="ffn_final" if final else "ffn",
    )(x2d, g, wg, wu, wd, g_final.reshape(1, D_MODEL))


def kernel(x, mem, positions, norm_mix, w_in, s5_lambda_re, s5_lambda_im, s5_log_step, s5_b_re, s5_b_im, s5_c_re, s5_c_im, s5_d, s5_w_glu, s5_b_glu, s5_out_norm, ret_out_norm, w_out, norm_cross, norm_mem, w_cq, w_ck, w_cv, w_co, norm_ffn, w_gate, w_up, w_down, norm_final):
    nb, seq, _ = x.shape
    n = nb * seq
    assert seq % RET_TR == 0 and seq % (S5_R * S5_TK) == 0 and seq % TM == 0
    bf = lambda a: a.astype(BF16)
    vec = lambda a: a.reshape(DEPTH, 1, a.shape[-1])

    rope = _rope_tables(positions)
    a_r, a_i, s5_bp, s5_w = _s5_discretise(
        s5_lambda_re, s5_lambda_im, s5_log_step, s5_b_re, s5_b_im, s5_c_re, s5_c_im)
    a_r = a_r.reshape(DEPTH, S5_KT, 1, S5_SLAB)
    a_i = a_i.reshape(DEPTH, S5_KT, 1, S5_SLAB)
    k_mem, v_mem = _mem_kv(mem.reshape(nb * MEM_LEN, D_MODEL), norm_mem, bf(w_ck), bf(w_cv))
    lg = jnp.log1p(-jnp.exp2(-5.0 - jnp.arange(RET_HEADS, dtype=F32)))
    lg = jnp.broadcast_to(lg[:, None, None], (RET_HEADS, 1, RET_DK))
    ret_gain = ret_out_norm.reshape(DEPTH, RET_HEADS, 1, RET_DK)
    w_in_b, w_out_b, w_cq_b, w_co_b = bf(w_in), bf(w_out), bf(w_cq * X_HEAD_DIM ** -0.5), bf(w_co)
    w_gate_b, w_up_b, w_down_b, w_glu_b = bf(w_gate), bf(w_up), bf(w_down), bf(s5_w_glu)
    g_mix, g_cross, g_ffn = vec(norm_mix), vec(norm_cross), vec(norm_ffn)
    s5_d3, s5_bglu3, s5_gout3 = vec(s5_d), vec(s5_b_glu), vec(s5_out_norm)

    x2d = x.reshape(n, D_MODEL)
    for l in range(DEPTH):
        u, ub, qkv, gate = _in_proj(l, x2d, g_mix, w_in_b, rope, nb, seq)
        y_ssm = _s5_mixer(l, u, ub, s5_bp, a_r, a_i, s5_w,
                          s5_d3, w_glu_b, s5_bglu3, s5_gout3)
        y_ret = _retention(l, qkv, gate, lg, ret_gain, nb, seq)
        x2d = _mix_out(l, x2d, y_ssm, y_ret, w_out_b, g_cross, w_cq_b,
                       k_mem, v_mem, w_co_b, seq // TM)
        x2d = _ffn(l, x2d, g_ffn, w_gate_b, w_up_b, w_down_b, norm_final, l == DEPTH - 1)
    return x2d.reshape(nb, seq, D_MODEL)
```
